```python
import jax, jax.numpy as jnp
from jax import lax
import numpy as np

D_MODEL = 1024
BATCH = 16
SEQ = 2048
DEPTH = 4

N_MIXERS = 4
EXPAND = 2
EXP_WIDTH = EXPAND * D_MODEL
CHUNK = 128
GM_HEADS = 8
GM_HEAD_DIM = EXP_WIDTH // GM_HEADS
CONV_WIDTH = 31
SHORT_CONV_WIDTH = 3
POOL_WINDOWS = (2, 4, 8, 16)
POOL_GROUP = EXP_WIDTH // len(POOL_WINDOWS)
EPS = 1e-6
N_GM = (DEPTH + 3) // N_MIXERS
N_CV = (DEPTH + 2) // N_MIXERS
N_SC = (DEPTH + 1) // N_MIXERS
N_PL = DEPTH // N_MIXERS

kernel_name = "hybrid_interleaved_conv_pool_gmlp_encoder"


def rmsnorm(x, g):
    xf = x.astype(jnp.float32)
    y = xf * lax.rsqrt(jnp.mean(xf * xf, axis=-1, keepdims=True) + EPS)
    return y.astype(x.dtype) * g


def layernorm(x, g, b):
    xf = x.astype(jnp.float32)
    mu = jnp.mean(xf, axis=-1, keepdims=True)
    xc = xf - mu
    y = xc * lax.rsqrt(jnp.mean(xc * xc, axis=-1, keepdims=True) + EPS)
    return y.astype(x.dtype) * g + b


def depthwise_conv(x, w, pad):
    return lax.conv_general_dilated(
        x, w[:, None, :], window_strides=(1,), padding=[(pad, pad)],
        dimension_numbers=("NWC", "WIO", "NWC"), feature_group_count=x.shape[-1])


def chunked_gmlp(h, w_in, ln_g, ln_b, sp_w, sp_b, w_out):
    bsz, seq, _ = h.shape
    u, v, z = jnp.split(h @ w_in, 3, axis=-1)
    u = jax.nn.gelu(u)
    v = layernorm(jax.nn.gelu(v), ln_g, ln_b)
    v = v.reshape(bsz, seq // CHUNK, CHUNK, GM_HEADS, GM_HEAD_DIM)
    v = jnp.einsum("hpq,bnqhc->bnphc", sp_w, v) + sp_b.T[None, None, :, :, None]
    v = v.reshape(bsz, seq, EXP_WIDTH)
    return (u * v * jax.nn.silu(z)) @ w_out


def conformer_conv(h, w_in, dw_w, dw_b, ln_g, ln_b, w_out):
    a, g, z = jnp.split(h @ w_in, 3, axis=-1)
    y = a * jax.nn.sigmoid(g)
    y = depthwise_conv(y, dw_w, CONV_WIDTH // 2) + dw_b
    y = jax.nn.silu(layernorm(y, ln_g, ln_b))
    return (y * jax.nn.silu(z)) @ w_out


def short_gated_conv(h, w_in, conv_w, w_out):
    bg, cg, v, z = jnp.split(h @ w_in, 4, axis=-1)
    y = bg * depthwise_conv(cg * v, conv_w, SHORT_CONV_WIDTH // 2)
    return (y * jax.nn.silu(z)) @ w_out


def multiscale_pool(h, w_in, pool_w, pool_b, scale, w_out):
    p, z = jnp.split(h @ w_in, 2, axis=-1)
    bsz, seq, _ = p.shape
    pf = p.astype(jnp.float32)
    cs = jnp.concatenate([jnp.zeros((bsz, 1, EXP_WIDTH), jnp.float32),
                          jnp.cumsum(pf, axis=1)], axis=1)
    t = jnp.arange(seq)
    outs = []
    for gi, w in enumerate(POOL_WINDOWS):
        sl = slice(gi * POOL_GROUP, (gi + 1) * POOL_GROUP)
        lo = jnp.clip(t - w // 2, 0, seq - 1)
        hi = jnp.clip(t + w - 1 - w // 2, 0, seq - 1)
        csg = cs[..., sl]
        win_sum = jnp.take(csg, hi + 1, axis=1) - jnp.take(csg, lo, axis=1)
        mean = win_sum / (hi - lo + 1).astype(jnp.float32)[None, :, None]
        d = (mean - pf[..., sl]).astype(p.dtype)
        outs.append(d @ pool_w[gi] + pool_b[gi])
    y = jnp.concatenate(outs, axis=-1) * scale
    return (y * jax.nn.silu(z)) @ w_out


def setup_inputs(seed: int = 0) -> dict:
    key = jax.random.key(seed)
    ks = iter(jax.random.split(key, 40))
    nrm = lambda shape, s: jax.random.normal(next(ks), shape, jnp.float32) * s
    D, E = D_MODEL, EXP_WIDTH
    return {
        "x": nrm((BATCH, SEQ, D), 1.0),
        "c": nrm((BATCH, D), 1.0),
        "norm_g": 1.0 + nrm((DEPTH, D), 0.02),
        "ada_w": nrm((DEPTH, D, 3 * D), 0.5 * D ** -0.5),
        "ada_b": nrm((DEPTH, 3 * D), 0.02),
        "gm_w_in": nrm((N_GM, D, 3 * E), D ** -0.5),
        "gm_ln_g": 1.0 + nrm((N_GM, E), 0.02),
        "gm_ln_b": nrm((N_GM, E), 0.02),
        "gm_sp_w": nrm((N_GM, GM_HEADS, CHUNK, CHUNK), CHUNK ** -0.5),
        "gm_sp_b": 1.0 + nrm((N_GM, GM_HEADS, CHUNK), 0.02),
        "gm_w_out": nrm((N_GM, E, D), E ** -0.5),
        "cv_w_in": nrm((N_CV, D, 3 * E), D ** -0.5),
        "cv_dw_w": nrm((N_CV, CONV_WIDTH, E), CONV_WIDTH ** -0.5),
        "cv_dw_b": nrm((N_CV, E), 0.02),
        "cv_ln_g": 1.0 + nrm((N_CV, E), 0.02),
        "cv_ln_b": nrm((N_CV, E), 0.02),
        "cv_w_out": nrm((N_CV, E, D), E ** -0.5),
        "sc_w_in": nrm((N_SC, D, 4 * E), D ** -0.5),
        "sc_conv_w": nrm((N_SC, SHORT_CONV_WIDTH, E), SHORT_CONV_WIDTH ** -0.5),
        "sc_w_out": nrm((N_SC, E, D), E ** -0.5),
        "pl_w_in": nrm((N_PL, D, 2 * E), D ** -0.5),
        "pl_w": nrm((N_PL, len(POOL_WINDOWS), POOL_GROUP, POOL_GROUP), POOL_GROUP ** -0.5),
        "pl_b": nrm((N_PL, len(POOL_WINDOWS), POOL_GROUP), 0.02),
        "pl_scale": 1.0 + nrm((N_PL, E), 0.1),
        "pl_w_out": nrm((N_PL, E, D), E ** -0.5),
        "final_g": 1.0 + nrm((D,), 0.02),
    }


def reference(x, c, norm_g, ada_w, ada_b,
              gm_w_in, gm_ln_g, gm_ln_b, gm_sp_w, gm_sp_b, gm_w_out,
              cv_w_in, cv_dw_w, cv_dw_b, cv_ln_g, cv_ln_b, cv_w_out,
              sc_w_in, sc_conv_w, sc_w_out,
              pl_w_in, pl_w, pl_b, pl_scale, pl_w_out,
              final_g):
    c_act = jax.nn.silu(c)
    for i in range(DEPTH):
        kind, j = i % N_MIXERS, i // N_MIXERS
        mod = c_act @ ada_w[i] + ada_b[i]
        shift, scale, gate = jnp.split(mod, 3, axis=-1)
        h = rmsnorm(x, norm_g[i]) * (1.0 + scale[:, None, :]) + shift[:, None, :]
        if kind == 0:
            y = chunked_gmlp(h, gm_w_in[j], gm_ln_g[j], gm_ln_b[j],
                             gm_sp_w[j], gm_sp_b[j], gm_w_out[j])
        elif kind == 1:
            y = conformer_conv(h, cv_w_in[j], cv_dw_w[j], cv_dw_b[j],
                               cv_ln_g[j], cv_ln_b[j], cv_w_out[j])
        elif kind == 2:
            y = short_gated_conv(h, sc_w_in[j], sc_conv_w[j], sc_w_out[j])
        else:
            y = multiscale_pool(h, pl_w_in[j], pl_w[j], pl_b[j], pl_scale[j], pl_w_out[j])
        x = x + gate[:, None, :] * y
    return rmsnorm(x, final_g)
```

```python
import functools
import math

import jax
import jax.numpy as jnp
from jax import lax
from jax.experimental import pallas as pl
from jax.experimental.pallas import tpu as pltpu

EPS = 1e-6
CHUNK = 128
GM_HEADS = 8
CONV_WIDTH = 31
SHORT_CONV_WIDTH = 3
POOL_WINDOWS = (2, 4, 8, 16)

TM = 512
HALO = 16
CB = 512
CONV_ROWS = 64
CONV_LANES = 256
VMEM_LIMIT_BYTES = 56 * 1024 * 1024

BF16 = jnp.bfloat16
F32 = jnp.float32


def _dot(a, b):
    return jnp.dot(a, b, preferred_element_type=F32)


def _sigmoid(x):
    return 1.0 / (1.0 + jnp.exp(-x))


def _silu(x):
    return x * _sigmoid(x)


def _gelu_tanh(x):
    c = math.sqrt(2.0 / math.pi)
    return 0.5 * x * (1.0 + jnp.tanh(c * (x + 0.044715 * (x * x * x))))


def _modulated_norm(x, ng, shift, scale):
    ms = jnp.mean(x * x, axis=-1, keepdims=True)
    return (x * lax.rsqrt(ms + EPS)) * ng * (1.0 + scale) + shift


def _rows_valid(rows, halo, seq):
    t = pl.program_id(1) * TM - halo + lax.broadcasted_iota(jnp.int32, (rows, 1), 0)
    return jnp.logical_and(t >= 0, t < seq)


def _ada_kernel(c_ref, w_ref, b_ref, o_ref):
    c_act = _silu(c_ref[...]).astype(BF16)
    o_ref[0] = _dot(c_act, w_ref[0].astype(BF16)) + b_ref[0]


def _ada_mod(c, ada_w, ada_b):
    depth, d, n = ada_w.shape
    bsz = c.shape[0]
    nb = d
    return pl.pallas_call(
        _ada_kernel,
        out_shape=jax.ShapeDtypeStruct((depth, bsz, n), F32),
        grid=(depth, n // nb),
        in_specs=[
            pl.BlockSpec((bsz, d), lambda i, j: (0, 0)),
            pl.BlockSpec((1, d, nb), lambda i, j: (i, 0, j)),
            pl.BlockSpec((1, 1, nb), lambda i, j: (i, 0, j)),
        ],
        out_specs=pl.BlockSpec((1, bsz, nb), lambda i, j: (i, 0, j)),
        compiler_params=pltpu.CompilerParams(
            dimension_semantics=("arbitrary", "arbitrary")),
        name="ada_mod",
    )(c, ada_w, ada_b.reshape(depth, 1, n))


def _fill_h(h_scr, x_refs, ng_ref, mod_ref):
    ng = ng_ref[...]
    shift = mod_ref[0, 0:1, :]
    scale = mod_ref[0, 1:2, :]
    r0 = 0
    for xr in x_refs:
        rows = xr.shape[1]
        h = _modulated_norm(xr[0], ng, shift, scale)
        h_scr[pl.ds(r0, rows), :] = h.astype(BF16)
        r0 += rows


def _residual(x_ref, mod_ref, acc):
    return x_ref[0] + mod_ref[0, 2:3, :] * acc


def _const_spec(shape):
    nd = len(shape)
    return pl.BlockSpec(shape, lambda b, s: (0,) * nd, pipeline_mode=pl.Buffered(1))


def _x_specs(d, seq, halo):
    main = pl.BlockSpec((1, TM, d), lambda b, s: (b, s, 0))
    if not halo:
        return [main]
    per_tile = TM // halo
    last = seq // halo - 1
    prev = pl.BlockSpec((1, halo, d),
                        lambda b, s: (b, jnp.maximum(s * per_tile - 1, 0), 0))
    nxt = pl.BlockSpec((1, halo, d),
                       lambda b, s: (b, jnp.minimum((s + 1) * per_tile, last), 0))
    return [prev, main, nxt]


def _layer_call(kernel, name, x, mod, ng, weights, scratch, halo):
    bsz, seq, d = x.shape
    x_specs = _x_specs(d, seq, halo)
    in_specs = (x_specs
                + [pl.BlockSpec((1, 3, d), lambda b, s: (b, 0, 0)),
                   _const_spec((1, d))]
                + [_const_spec(w.shape) for w in weights])
    return pl.pallas_call(
        kernel,
        out_shape=jax.ShapeDtypeStruct(x.shape, x.dtype),
        grid=(bsz, seq // TM),
        in_specs=in_specs,
        out_specs=pl.BlockSpec((1, TM, d), lambda b, s: (b, s, 0)),
        scratch_shapes=scratch,
        compiler_params=pltpu.CompilerParams(
            dimension_semantics=("arbitrary", "arbitrary"),
            vmem_limit_bytes=VMEM_LIMIT_BYTES),
        name=name,
    )(*([x] * len(x_specs)), mod, ng.reshape(1, d), *weights)


def _split_cols(w, blk):
    d, n = w.shape
    return w.astype(BF16).reshape(d, n // blk, blk).transpose(1, 0, 2)


def _split_rows(w, blk):
    e, d = w.shape
    return w.astype(BF16).reshape(e // blk, blk, d)


def _split_vec(v, blk):
    k, e = v.shape
    return v.reshape(k, e // blk, blk).transpose(1, 0, 2)


def _gmlp_kernel(x_ref, mod_ref, ng_ref, w_u_ref, w_v_ref, w_z_ref, lng_ref, lnb_ref,
                 spw_ref, spb_ref, w_out_ref, o_ref, h_scr, v_scr, acc_scr):
    n_heads, _, hd = w_u_ref.shape
    e = n_heads * hd
    _fill_h(h_scr, [x_ref], ng_ref, mod_ref)

    def v_body(g, carry):
        s1, s2 = carry
        v = _gelu_tanh(_dot(h_scr[...], w_v_ref[g]))
        v_scr[g] = v
        return (s1 + jnp.sum(v, axis=-1, keepdims=True),
                s2 + jnp.sum(v * v, axis=-1, keepdims=True))

    zero = jnp.zeros((TM, 1), F32)
    s1, s2 = lax.fori_loop(0, n_heads, v_body, (zero, zero))
    mu = s1 / e
    rstd = lax.rsqrt(s2 / e - mu * mu + EPS)

    acc_scr[...] = jnp.zeros_like(acc_scr)

    def head_body(g, carry):
        vn = ((v_scr[g] - mu) * rstd * lng_ref[g] + lnb_ref[g]).astype(BF16)
        spw = spw_ref[g]
        spb = spb_ref[g]
        mixed = [_dot(spw, vn[n * CHUNK:(n + 1) * CHUNK, :]) + spb
                 for n in range(TM // CHUNK)]
        vs = jnp.concatenate(mixed, axis=0)
        hx = h_scr[...]
        u = _gelu_tanh(_dot(hx, w_u_ref[g]))
        z = _dot(hx, w_z_ref[g])
        y = (u * vs * _silu(z)).astype(BF16)
        acc_scr[...] += _dot(y, w_out_ref[g])
        return carry

    lax.fori_loop(0, n_heads, head_body, 0)
    o_ref[0] = _residual(x_ref, mod_ref, acc_scr[...])


def _gmlp_layer(x, mod, ng, w_in, ln_g, ln_b, sp_w, sp_b, w_out):
    d = x.shape[-1]
    e = w_out.shape[0]
    hd = e // GM_HEADS
    slabs = _split_cols(w_in, hd)
    weights = [
        slabs[:GM_HEADS], slabs[GM_HEADS:2 * GM_HEADS], slabs[2 * GM_HEADS:],
        ln_g.reshape(GM_HEADS, 1, hd), ln_b.reshape(GM_HEADS, 1, hd),
        sp_w.astype(BF16), sp_b.reshape(GM_HEADS, CHUNK, 1),
        _split_rows(w_out, hd),
    ]
    scratch = [pltpu.VMEM((TM, d), BF16),
               pltpu.VMEM((GM_HEADS, TM, hd), F32),
               pltpu.VMEM((TM, d), F32)]
    return _layer_call(_gmlp_kernel, "gmlp_layer", x, mod, ng, weights, scratch, 0)


def _depthwise_conv_rows(y_scr, dw_ref, j, bias, first_row):
    cols = []
    for l0 in range(0, CB, CONV_LANES):
        blocks = []
        for r0 in range(0, TM, CONV_ROWS):
            acc = jnp.broadcast_to(bias[:, l0:l0 + CONV_LANES], (CONV_ROWS, CONV_LANES))
            for k in range(CONV_WIDTH):
                w = dw_ref[j, pl.ds(k, 1), pl.ds(l0, CONV_LANES)]
                acc = acc + w * y_scr[pl.ds(first_row + r0 + k, CONV_ROWS),
                                      pl.ds(l0, CONV_LANES)]
            blocks.append(acc)
        cols.append(jnp.concatenate(blocks, axis=0))
    return jnp.concatenate(cols, axis=1)


def _conformer_kernel(xp_ref, x_ref, xn_ref, mod_ref, ng_ref, w_a_ref, w_g_ref, w_z_ref,
                      dw_ref, dwb_ref, lng_ref, lnb_ref, w_out_ref, o_ref,
                      h_scr, y_scr, c_scr, acc_scr, *, seq):
    nblk = w_a_ref.shape[0]
    e = nblk * CB
    rows = TM + 2 * HALO
    _fill_h(h_scr, [xp_ref, x_ref, xn_ref], ng_ref, mod_ref)
    valid = _rows_valid(rows, HALO, seq)
    pad = CONV_WIDTH // 2

    def conv_body(j, carry):
        s1, s2 = carry
        hx = h_scr[...]
        a = _dot(hx, w_a_ref[j])
        g = _dot(hx, w_g_ref[j])
        y_scr[...] = jnp.where(valid, a * _sigmoid(g), 0.0)
        c = _depthwise_conv_rows(y_scr, dw_ref, j, dwb_ref[j], HALO - pad)
        c_scr[j] = c
        return (s1 + jnp.sum(c, axis=-1, keepdims=True),
                s2 + jnp.sum(c * c, axis=-1, keepdims=True))

    zero = jnp.zeros((TM, 1), F32)
    s1, s2 = lax.fori_loop(0, nblk, conv_body, (zero, zero))
    mu = s1 / e
    rstd = lax.rsqrt(s2 / e - mu * mu + EPS)

    acc_scr[...] = jnp.zeros_like(acc_scr)

    def out_body(j, carry):
        cn = (c_scr[j] - mu) * rstd * lng_ref[j] + lnb_ref[j]
        z = _dot(h_scr[pl.ds(HALO, TM), :], w_z_ref[j])
        y = (_silu(cn) * _silu(z)).astype(BF16)
        acc_scr[...] += _dot(y, w_out_ref[j])
        return carry

    lax.fori_loop(0, nblk, out_body, 0)
    o_ref[0] = _residual(x_ref, mod_ref, acc_scr[...])


def _conformer_layer(x, mod, ng, w_in, dw_w, dw_b, ln_g, ln_b, w_out):
    d = x.shape[-1]
    seq = x.shape[1]
    e = w_out.shape[0]
    nblk = e // CB
    slabs = _split_cols(w_in, CB)
    weights = [
        slabs[:nblk], slabs[nblk:2 * nblk], slabs[2 * nblk:],
        _split_vec(dw_w, CB), dw_b.reshape(nblk, 1, CB),
        ln_g.reshape(nblk, 1, CB), ln_b.reshape(nblk, 1, CB),
        _split_rows(w_out, CB),
    ]
    rows = TM + 2 * HALO
    scratch = [pltpu.VMEM((rows, d), BF16),
               pltpu.VMEM((rows, CB), F32),
               pltpu.VMEM((nblk, TM, CB), F32),
               pltpu.VMEM((TM, d), F32)]
    kernel = functools.partial(_conformer_kernel, seq=seq)
    return _layer_call(kernel, "conformer_layer", x, mod, ng, weights, scratch, HALO)


def _shortconv_kernel(xp_ref, x_ref, xn_ref, mod_ref, ng_ref, w_b_ref, w_c_ref, w_v_ref,
                      w_z_ref, cw_ref, w_out_ref, o_ref, h_scr, q_scr, acc_scr, *, seq):
    nblk = w_b_ref.shape[0]
    rows = TM + 2 * HALO
    _fill_h(h_scr, [xp_ref, x_ref, xn_ref], ng_ref, mod_ref)
    valid = _rows_valid(rows, HALO, seq)
    pad = SHORT_CONV_WIDTH // 2
    acc_scr[...] = jnp.zeros_like(acc_scr)

    def body(j, carry):
        hx = h_scr[...]
        q_scr[...] = jnp.where(valid, _dot(hx, w_c_ref[j]) * _dot(hx, w_v_ref[j]), 0.0)
        cw = cw_ref[j]
        conv = cw[0:1, :] * q_scr[pl.ds(HALO - pad, TM), :]
        for k in range(1, SHORT_CONV_WIDTH):
            conv = conv + cw[k:k + 1, :] * q_scr[pl.ds(HALO - pad + k, TM), :]
        hm = h_scr[pl.ds(HALO, TM), :]
        bg = _dot(hm, w_b_ref[j])
        z = _dot(hm, w_z_ref[j])
        y = (bg * conv * _silu(z)).astype(BF16)
        acc_scr[...] += _dot(y, w_out_ref[j])
        return carry

    lax.fori_loop(0, nblk, body, 0)
    o_ref[0] = _residual(x_ref, mod_ref, acc_scr[...])


def _shortconv_layer(x, mod, ng, w_in, conv_w, w_out):
    d = x.shape[-1]
    seq = x.shape[1]
    e = w_out.shape[0]
    nblk = e // CB
    slabs = _split_cols(w_in, CB)
    weights = [
        slabs[:nblk], slabs[nblk:2 * nblk], slabs[2 * nblk:3 * nblk], slabs[3 * nblk:],
        _split_vec(conv_w, CB),
        _split_rows(w_out, CB),
    ]
    rows = TM + 2 * HALO
    scratch = [pltpu.VMEM((rows, d), BF16),
               pltpu.VMEM((rows, CB), F32),
               pltpu.VMEM((TM, d), F32)]
    kernel = functools.partial(_shortconv_kernel, seq=seq)
    return _layer_call(kernel, "shortconv_layer", x, mod, ng, weights, scratch, HALO)


def _pool_kernel(xp_ref, x_ref, xn_ref, mod_ref, ng_ref, w_p_ref, w_z_ref, pw_ref, pb_ref,
                 sc_ref, w_out_ref, fg_ref, o_ref, h_scr, p_scr, acc_scr, *, seq):
    rows = TM + 2 * HALO
    _fill_h(h_scr, [xp_ref, x_ref, xn_ref], ng_ref, mod_ref)
    valid = _rows_valid(rows, HALO, seq)
    t = pl.program_id(1) * TM + lax.broadcasted_iota(jnp.int32, (TM, 1), 0)
    acc_scr[...] = jnp.zeros_like(acc_scr)

    for gi, w in enumerate(POOL_WINDOWS):
        lo = jnp.maximum(t - w // 2, 0)
        hi = jnp.minimum(t + w - 1 - w // 2, seq - 1)
        inv_cnt = 1.0 / (hi - lo + 1).astype(F32)
        p_scr[...] = jnp.where(valid, _dot(h_scr[...], w_p_ref[gi]), 0.0)
        win = p_scr[pl.ds(HALO - w // 2, TM), :]
        for k in range(1, w):
            win = win + p_scr[pl.ds(HALO - w // 2 + k, TM), :]
        dlt = (win * inv_cnt - p_scr[pl.ds(HALO, TM), :]).astype(BF16)
        pooled = (_dot(dlt, pw_ref[gi]) + pb_ref[gi]) * sc_ref[gi]
        z = _dot(h_scr[pl.ds(HALO, TM), :], w_z_ref[gi])
        y = (pooled * _silu(z)).astype(BF16)
        acc_scr[...] += _dot(y, w_out_ref[gi])

    out = _residual(x_ref, mod_ref, acc_scr[...])
    ms = jnp.mean(out * out, axis=-1, keepdims=True)
    o_ref[0] = (out * lax.rsqrt(ms + EPS)) * fg_ref[...]


def _pool_layer(x, mod, ng, w_in, pool_w, pool_b, scale, w_out, final_g):
    d = x.shape[-1]
    seq = x.shape[1]
    e = w_out.shape[0]
    ngrp = len(POOL_WINDOWS)
    grp = e // ngrp
    slabs = _split_cols(w_in, grp)
    weights = [
        slabs[:ngrp], slabs[ngrp:],
        pool_w.astype(BF16), pool_b.reshape(ngrp, 1, grp), scale.reshape(ngrp, 1, grp),
        _split_rows(w_out, grp),
        final_g.reshape(1, d),
    ]
    rows = TM + 2 * HALO
    scratch = [pltpu.VMEM((rows, d), BF16),
               pltpu.VMEM((rows, grp), F32),
               pltpu.VMEM((TM, d), F32)]
    kernel = functools.partial(_pool_kernel, seq=seq)
    return _layer_call(kernel, "pool_layer", x, mod, ng, weights, scratch, HALO)


def kernel(x, c, norm_g, ada_w, ada_b, gm_w_in, gm_ln_g, gm_ln_b, gm_sp_w, gm_sp_b, gm_w_out, cv_w_in, cv_dw_w, cv_dw_b, cv_ln_g, cv_ln_b, cv_w_out, sc_w_in, sc_conv_w, sc_w_out, pl_w_in, pl_w, pl_b, pl_scale, pl_w_out, final_g):
    bsz, seq, d = x.shape
    assert seq % TM == 0 and TM % CHUNK == 0 and TM % HALO == 0
    assert norm_g.shape[0] == 4, "one layer of each mixer kind, final norm fused in the last"
    mod = _ada_mod(c, ada_w, ada_b).reshape(norm_g.shape[0], bsz, 3, d)
    x = _gmlp_layer(x, mod[0], norm_g[0], gm_w_in[0], gm_ln_g[0], gm_ln_b[0],
                    gm_sp_w[0], gm_sp_b[0], gm_w_out[0])
    x = _conformer_layer(x, mod[1], norm_g[1], cv_w_in[0], cv_dw_w[0], cv_dw_b[0],
                         cv_ln_g[0], cv_ln_b[0], cv_w_out[0])
    x = _shortconv_layer(x, mod[2], norm_g[2], sc_w_in[0], sc_conv_w[0], sc_w_out[0])
    return _pool_layer(x, mod[3], norm_g[3], pl_w_in[0], pl_w[0], pl_b[0], pl_scale[0],
                       pl_w_out[0], final_g)
```

```python
import functools
import math

import jax
import jax.numpy as jnp
from jax import lax
from jax.experimental import pallas as pl
from jax.experimental.pallas import tpu as pltpu

EPS = 1e-6
CHUNK = 128
GM_HEADS = 8
CONV_WIDTH = 31
SHORT_CONV_WIDTH = 3
POOL_WINDOWS = (2, 4, 8, 16)

TM = 512
HALO = 16
CB = 512
CONV_ROWS = 64
LANES = 128
VMEM_LIMIT_BYTES = 56 * 1024 * 1024

BF16 = jnp.bfloat16
F32 = jnp.float32


def _dot(a, b):
    return jnp.dot(a, b, preferred_element_type=F32)


def _sigmoid(x):
    return 1.0 / (1.0 + jnp.exp(-x))


def _silu(x):
    return x * _sigmoid(x)


def _gelu_tanh(x):
    c = math.sqrt(2.0 / math.pi)
    return 0.5 * x * (1.0 + jnp.tanh(c * (x + 0.044715 * (x * x * x))))


def _modulated_norm(x, ng, shift, scale):
    ms = jnp.mean(x * x, axis=-1, keepdims=True)
    return (x * lax.rsqrt(ms + EPS)) * ng * (1.0 + scale) + shift


def _rows_valid(rows, halo, seq):
    t = pl.program_id(1) * TM - halo + lax.broadcasted_iota(jnp.int32, (rows, 1), 0)
    return jnp.logical_and(t >= 0, t < seq)


def _ada_kernel(c_ref, w_ref, b_ref, o_ref):
    c_act = _silu(c_ref[...]).astype(BF16)
    o_ref[0] = _dot(c_act, w_ref[0].astype(BF16)) + b_ref[0]


def _ada_mod(c, ada_w, ada_b):
    depth, d, n = ada_w.shape
    bsz = c.shape[0]
    nb = d
    return pl.pallas_call(
        _ada_kernel,
        out_shape=jax.ShapeDtypeStruct((depth, bsz, n), F32),
        grid=(depth, n // nb),
        in_specs=[
            pl.BlockSpec((bsz, d), lambda i, j: (0, 0)),
            pl.BlockSpec((1, d, nb), lambda i, j: (i, 0, j)),
            pl.BlockSpec((1, 1, nb), lambda i, j: (i, 0, j)),
        ],
        out_specs=pl.BlockSpec((1, bsz, nb), lambda i, j: (i, 0, j)),
        compiler_params=pltpu.CompilerParams(
            dimension_semantics=("arbitrary", "arbitrary")),
        name="ada_mod",
    )(c, ada_w, ada_b.reshape(depth, 1, n))


def _fill_h(h_scr, x_refs, ng_ref, mod_ref):
    ng = ng_ref[...]
    shift = mod_ref[0, 0:1, :]
    scale = mod_ref[0, 1:2, :]
    r0 = 0
    for xr in x_refs:
        rows = xr.shape[1]
        h = _modulated_norm(xr[0], ng, shift, scale)
        h_scr[pl.ds(r0, rows), :] = h.astype(BF16)
        r0 += rows


def _residual(x_ref, mod_ref, acc):
    return x_ref[0] + mod_ref[0, 2:3, :] * acc


def _store_slabs(scr, val):
    for s in range(scr.shape[0]):
        scr[s] = val[:, s * LANES:(s + 1) * LANES]


def _shifted(scr, first_row):
    return jnp.concatenate(
        [scr[s, pl.ds(first_row, TM), :] for s in range(scr.shape[0])], axis=1)


def _const_spec(shape):
    nd = len(shape)
    return pl.BlockSpec(shape, lambda b, s: (0,) * nd, pipeline_mode=pl.Buffered(1))


def _x_specs(d, seq, halo):
    main = pl.BlockSpec((1, TM, d), lambda b, s: (b, s, 0))
    if not halo:
        return [main]
    per_tile = TM // halo
    last = seq // halo - 1
    prev = pl.BlockSpec((1, halo, d),
                        lambda b, s: (b, jnp.maximum(s * per_tile - 1, 0), 0))
    nxt = pl.BlockSpec((1, halo, d),
                       lambda b, s: (b, jnp.minimum((s + 1) * per_tile, last), 0))
    return [prev, main, nxt]


def _layer_call(kernel, name, x, mod, ng, weights, scratch, halo):
    bsz, seq, d = x.shape
    x_specs = _x_specs(d, seq, halo)
    in_specs = (x_specs
                + [pl.BlockSpec((1, 3, d), lambda b, s: (b, 0, 0)),
                   _const_spec((1, d))]
                + [_const_spec(w.shape) for w in weights])
    return pl.pallas_call(
        kernel,
        out_shape=jax.ShapeDtypeStruct(x.shape, x.dtype),
        grid=(bsz, seq // TM),
        in_specs=in_specs,
        out_specs=pl.BlockSpec((1, TM, d), lambda b, s: (b, s, 0)),
        scratch_shapes=scratch,
        compiler_params=pltpu.CompilerParams(
            dimension_semantics=("arbitrary", "arbitrary"),
            vmem_limit_bytes=VMEM_LIMIT_BYTES),
        name=name,
    )(*([x] * len(x_specs)), mod, ng.reshape(1, d), *weights)


def _split_cols(w, blk):
    d, n = w.shape
    return w.astype(BF16).reshape(d, n // blk, blk).transpose(1, 0, 2)


def _split_rows(w, blk):
    e, d = w.shape
    return w.astype(BF16).reshape(e // blk, blk, d)


def _split_vec(v, blk):
    k, e = v.shape
    return v.reshape(k, e // blk, blk).transpose(1, 0, 2)


def _gmlp_kernel(x_ref, mod_ref, ng_ref, w_u_ref, w_v_ref, w_z_ref, lng_ref, lnb_ref,
                 spw_ref, spb_ref, w_out_ref, o_ref, h_scr, v_scr, acc_scr):
    n_heads, _, hd = w_u_ref.shape
    e = n_heads * hd
    _fill_h(h_scr, [x_ref], ng_ref, mod_ref)

    def v_body(g, carry):
        s1, s2 = carry
        v = _gelu_tanh(_dot(h_scr[...], w_v_ref[g]))
        v_scr[g] = v
        return (s1 + jnp.sum(v, axis=-1, keepdims=True),
                s2 + jnp.sum(v * v, axis=-1, keepdims=True))

    zero = jnp.zeros((TM, 1), F32)
    s1, s2 = lax.fori_loop(0, n_heads, v_body, (zero, zero))
    mu = s1 / e
    rstd = lax.rsqrt(s2 / e - mu * mu + EPS)

    acc_scr[...] = jnp.zeros_like(acc_scr)

    def head_body(g, carry):
        vn = ((v_scr[g] - mu) * rstd * lng_ref[g] + lnb_ref[g]).astype(BF16)
        spw = spw_ref[g]
        spb = spb_ref[g]
        mixed = [_dot(spw, vn[n * CHUNK:(n + 1) * CHUNK, :]) + spb
                 for n in range(TM // CHUNK)]
        vs = jnp.concatenate(mixed, axis=0)
        hx = h_scr[...]
        u = _gelu_tanh(_dot(hx, w_u_ref[g]))
        z = _dot(hx, w_z_ref[g])
        y = (u * vs * _silu(z)).astype(BF16)
        acc_scr[...] += _dot(y, w_out_ref[g])
        return carry

    lax.fori_loop(0, n_heads, head_body, 0)
    o_ref[0] = _residual(x_ref, mod_ref, acc_scr[...])


def _gmlp_layer(x, mod, ng, w_in, ln_g, ln_b, sp_w, sp_b, w_out):
    d = x.shape[-1]
    e = w_out.shape[0]
    hd = e // GM_HEADS
    slabs = _split_cols(w_in, hd)
    weights = [
        slabs[:GM_HEADS], slabs[GM_HEADS:2 * GM_HEADS], slabs[2 * GM_HEADS:],
        ln_g.reshape(GM_HEADS, 1, hd), ln_b.reshape(GM_HEADS, 1, hd),
        sp_w.astype(BF16), sp_b.reshape(GM_HEADS, CHUNK, 1),
        _split_rows(w_out, hd),
    ]
    scratch = [pltpu.VMEM((TM, d), BF16),
               pltpu.VMEM((GM_HEADS, TM, hd), F32),
               pltpu.VMEM((TM, d), F32)]
    return _layer_call(_gmlp_kernel, "gmlp_layer", x, mod, ng, weights, scratch, 0)


def _depthwise_conv(y_scr, dw_ref, dwb_ref, j, first_row, c_scr):
    for s in range(y_scr.shape[0]):
        lanes = pl.ds(s * LANES, LANES)
        bias = dwb_ref[j, :, lanes]
        for r0 in range(0, TM, CONV_ROWS):
            acc = jnp.broadcast_to(bias, (CONV_ROWS, LANES))
            for k in range(CONV_WIDTH):
                acc = acc + (dw_ref[j, pl.ds(k, 1), lanes]
                             * y_scr[s, pl.ds(first_row + r0 + k, CONV_ROWS), :])
            c_scr[j, pl.ds(r0, CONV_ROWS), lanes] = acc


def _conformer_kernel(xp_ref, x_ref, xn_ref, mod_ref, ng_ref, w_a_ref, w_g_ref, w_z_ref,
                      dw_ref, dwb_ref, lng_ref, lnb_ref, w_out_ref, o_ref,
                      h_scr, y_scr, c_scr, acc_scr, *, seq):
    nblk = w_a_ref.shape[0]
    e = nblk * CB
    rows = TM + 2 * HALO
    _fill_h(h_scr, [xp_ref, x_ref, xn_ref], ng_ref, mod_ref)
    valid = _rows_valid(rows, HALO, seq)
    pad = CONV_WIDTH // 2

    def conv_body(j, carry):
        s1, s2 = carry
        hx = h_scr[...]
        a = _dot(hx, w_a_ref[j])
        g = _dot(hx, w_g_ref[j])
        _store_slabs(y_scr, jnp.where(valid, a * _sigmoid(g), 0.0))
        _depthwise_conv(y_scr, dw_ref, dwb_ref, j, HALO - pad, c_scr)
        c = c_scr[j]
        return (s1 + jnp.sum(c, axis=-1, keepdims=True),
                s2 + jnp.sum(c * c, axis=-1, keepdims=True))

    zero = jnp.zeros((TM, 1), F32)
    s1, s2 = lax.fori_loop(0, nblk, conv_body, (zero, zero))
    mu = s1 / e
    rstd = lax.rsqrt(s2 / e - mu * mu + EPS)

    acc_scr[...] = jnp.zeros_like(acc_scr)

    def out_body(j, carry):
        cn = (c_scr[j] - mu) * rstd * lng_ref[j] + lnb_ref[j]
        z = _dot(h_scr[pl.ds(HALO, TM), :], w_z_ref[j])
        y = (_silu(cn) * _silu(z)).astype(BF16)
        acc_scr[...] += _dot(y, w_out_ref[j])
        return carry

    lax.fori_loop(0, nblk, out_body, 0)
    o_ref[0] = _residual(x_ref, mod_ref, acc_scr[...])


def _conformer_layer(x, mod, ng, w_in, dw_w, dw_b, ln_g, ln_b, w_out):
    d = x.shape[-1]
    seq = x.shape[1]
    e = w_out.shape[0]
    nblk = e // CB
    slabs = _split_cols(w_in, CB)
    weights = [
        slabs[:nblk], slabs[nblk:2 * nblk], slabs[2 * nblk:],
        _split_vec(dw_w, CB), dw_b.reshape(nblk, 1, CB),
        ln_g.reshape(nblk, 1, CB), ln_b.reshape(nblk, 1, CB),
        _split_rows(w_out, CB),
    ]
    rows = TM + 2 * HALO
    scratch = [pltpu.VMEM((rows, d), BF16),
               pltpu.VMEM((CB // LANES, rows, LANES), F32),
               pltpu.VMEM((nblk, TM, CB), F32),
               pltpu.VMEM((TM, d), F32)]
    kernel = functools.partial(_conformer_kernel, seq=seq)
    return _layer_call(kernel, "conformer_layer", x, mod, ng, weights, scratch, HALO)


def _shortconv_kernel(xp_ref, x_ref, xn_ref, mod_ref, ng_ref, w_b_ref, w_c_ref, w_v_ref,
                      w_z_ref, cw_ref, w_out_ref, o_ref, h_scr, q_scr, acc_scr, *, seq):
    nblk = w_b_ref.shape[0]
    rows = TM + 2 * HALO
    _fill_h(h_scr, [xp_ref, x_ref, xn_ref], ng_ref, mod_ref)
    valid = _rows_valid(rows, HALO, seq)
    pad = SHORT_CONV_WIDTH // 2
    acc_scr[...] = jnp.zeros_like(acc_scr)

    def body(j, carry):
        hx = h_scr[...]
        _store_slabs(q_scr,
                     jnp.where(valid, _dot(hx, w_c_ref[j]) * _dot(hx, w_v_ref[j]), 0.0))
        cw = cw_ref[j]
        conv = cw[0:1, :] * _shifted(q_scr, HALO - pad)
        for k in range(1, SHORT_CONV_WIDTH):
            conv = conv + cw[k:k + 1, :] * _shifted(q_scr, HALO - pad + k)
        hm = h_scr[pl.ds(HALO, TM), :]
        bg = _dot(hm, w_b_ref[j])
        z = _dot(hm, w_z_ref[j])
        y = (bg * conv * _silu(z)).astype(BF16)
        acc_scr[...] += _dot(y, w_out_ref[j])
        return carry

    lax.fori_loop(0, nblk, body, 0)
    o_ref[0] = _residual(x_ref, mod_ref, acc_scr[...])


def _shortconv_layer(x, mod, ng, w_in, conv_w, w_out):
    d = x.shape[-1]
    seq = x.shape[1]
    e = w_out.shape[0]
    nblk = e // CB
    slabs = _split_cols(w_in, CB)
    weights = [
        slabs[:nblk], slabs[nblk:2 * nblk], slabs[2 * nblk:3 * nblk], slabs[3 * nblk:],
        _split_vec(conv_w, CB),
        _split_rows(w_out, CB),
    ]
    rows = TM + 2 * HALO
    scratch = [pltpu.VMEM((rows, d), BF16),
               pltpu.VMEM((CB // LANES, rows, LANES), F32),
               pltpu.VMEM((TM, d), F32)]
    kernel = functools.partial(_shortconv_kernel, seq=seq)
    return _layer_call(kernel, "shortconv_layer", x, mod, ng, weights, scratch, HALO)


def _pool_kernel(xp_ref, x_ref, xn_ref, mod_ref, ng_ref, w_p_ref, w_z_ref, pw_ref, pb_ref,
                 sc_ref, w_out_ref, fg_ref, o_ref, h_scr, p_scr, acc_scr, *, seq):
    rows = TM + 2 * HALO
    _fill_h(h_scr, [xp_ref, x_ref, xn_ref], ng_ref, mod_ref)
    valid = _rows_valid(rows, HALO, seq)
    t = pl.program_id(1) * TM + lax.broadcasted_iota(jnp.int32, (TM, 1), 0)
    acc_scr[...] = jnp.zeros_like(acc_scr)

    for gi, w in enumerate(POOL_WINDOWS):
        lo = jnp.maximum(t - w // 2, 0)
        hi = jnp.minimum(t + w - 1 - w // 2, seq - 1)
        inv_cnt = 1.0 / (hi - lo + 1).astype(F32)
        _store_slabs(p_scr, jnp.where(valid, _dot(h_scr[...], w_p_ref[gi]), 0.0))
        win = _shifted(p_scr, HALO - w // 2)
        for k in range(1, w):
            win = win + _shifted(p_scr, HALO - w // 2 + k)
        dlt = (win * inv_cnt - _shifted(p_scr, HALO)).astype(BF16)
        pooled = (_dot(dlt, pw_ref[gi]) + pb_ref[gi]) * sc_ref[gi]
        z = _dot(h_scr[pl.ds(HALO, TM), :], w_z_ref[gi])
        y = (pooled * _silu(z)).astype(BF16)
        acc_scr[...] += _dot(y, w_out_ref[gi])

    out = _residual(x_ref, mod_ref, acc_scr[...])
    ms = jnp.mean(out * out, axis=-1, keepdims=True)
    o_ref[0] = (out * lax.rsqrt(ms + EPS)) * fg_ref[...]


def _pool_layer(x, mod, ng, w_in, pool_w, pool_b, scale, w_out, final_g):
    d = x.shape[-1]
    seq = x.shape[1]
    e = w_out.shape[0]
    ngrp = len(POOL_WINDOWS)
    grp = e // ngrp
    slabs = _split_cols(w_in, grp)
    weights = [
        slabs[:ngrp], slabs[ngrp:],
        pool_w.astype(BF16), pool_b.reshape(ngrp, 1, grp), scale.reshape(ngrp, 1, grp),
        _split_rows(w_out, grp),
        final_g.reshape(1, d),
    ]
    rows = TM + 2 * HALO
    scratch = [pltpu.VMEM((rows, d), BF16),
               pltpu.VMEM((grp // LANES, rows, LANES), F32),
               pltpu.VMEM((TM, d), F32)]
    kernel = functools.partial(_pool_kernel, seq=seq)
    return _layer_call(kernel, "pool_layer", x, mod, ng, weights, scratch, HALO)


def kernel(x, c, norm_g, ada_w, ada_b, gm_w_in, gm_ln_g, gm_ln_b, gm_sp_w, gm_sp_b, gm_w_out, cv_w_in, cv_dw_w, cv_dw_b, cv_ln_g, cv_ln_b, cv_w_out, sc_w_in, sc_conv_w, sc_w_out, pl_w_in, pl_w, pl_b, pl_scale, pl_w_out, final_g):
    bsz, seq, d = x.shape
    assert seq % TM == 0 and TM % CHUNK == 0 and TM % HALO == 0
    assert norm_g.shape[0] == 4, "one layer of each mixer kind, final norm fused in the last"
    mod = _ada_mod(c, ada_w, ada_b).reshape(norm_g.shape[0], bsz, 3, d)
    x = _gmlp_layer(x, mod[0], norm_g[0], gm_w_in[0], gm_ln_g[0], gm_ln_b[0],
                    gm_sp_w[0], gm_sp_b[0], gm_w_out[0])
    x = _conformer_layer(x, mod[1], norm_g[1], cv_w_in[0], cv_dw_w[0], cv_dw_b[0],
                         cv_ln_g[0], cv_ln_b[0], cv_w_out[0])
    x = _shortconv_layer(x, mod[2], norm_g[2], sc_w_in[0], sc_conv_w[0], sc_w_out[0])
    return _pool_layer(x, mod[3], norm_g[3], pl_w_in[0], pl_w[0], pl_b[0], pl_scale[0],
                       pl_w_out[0], final_g)
```

```python
import functools
import math

import jax
import jax.numpy as jnp
from jax import lax
from jax.experimental import pallas as pl
from jax.experimental.pallas import tpu as pltpu

EPS = 1e-6
CHUNK = 128
GM_HEADS = 8
CONV_WIDTH = 31
SHORT_CONV_WIDTH = 3
POOL_WINDOWS = (2, 4, 8, 16)

TM = 512
HALO = 16
CB = 512
CONV_ROWS = 64
ROW_CHUNK = 256
LANES = 128
VMEM_LIMIT_BYTES = 56 * 1024 * 1024

BF16 = jnp.bfloat16
F32 = jnp.float32


def _dot(a, b):
    return jnp.dot(a, b, preferred_element_type=F32)


def _sigmoid(x):
    return 1.0 / (1.0 + jnp.exp(-x))


def _silu(x):
    return x * _sigmoid(x)


def _gelu_tanh(x):
    c = math.sqrt(2.0 / math.pi)
    return 0.5 * x * (1.0 + jnp.tanh(c * (x + 0.044715 * (x * x * x))))


def _modulated_norm(x, ng, shift, scale):
    ms = jnp.mean(x * x, axis=-1, keepdims=True)
    return (x * lax.rsqrt(ms + EPS)) * ng * (1.0 + scale) + shift


def _rows_valid(rows, halo, seq, tile_in_seq):
    t = tile_in_seq * TM - halo + lax.broadcasted_iota(jnp.int32, (rows, 1), 0)
    return jnp.logical_and(t >= 0, t < seq)


def _cols(w_ref, start, width):
    return w_ref[:, pl.ds(pl.multiple_of(start, width), width)]


def _ada_kernel(c_ref, w_ref, b_ref, o_ref):
    c_act = _silu(c_ref[...]).astype(BF16)
    o_ref[0] = _dot(c_act, w_ref[0].astype(BF16)) + b_ref[0]


def _ada_mod(c, ada_w, ada_b):
    depth, d, n = ada_w.shape
    bsz = c.shape[0]
    nb = d
    return pl.pallas_call(
        _ada_kernel,
        out_shape=jax.ShapeDtypeStruct((depth, bsz, n), F32),
        grid=(depth, n // nb),
        in_specs=[
            pl.BlockSpec((bsz, d), lambda i, j: (0, 0)),
            pl.BlockSpec((1, d, nb), lambda i, j: (i, 0, j)),
            pl.BlockSpec((1, 1, nb), lambda i, j: (i, 0, j)),
        ],
        out_specs=pl.BlockSpec((1, bsz, nb), lambda i, j: (i, 0, j)),
        compiler_params=pltpu.CompilerParams(
            dimension_semantics=("arbitrary", "arbitrary")),
        name="ada_mod",
    )(c, ada_w, ada_b.reshape(depth, 1, n))


def _fill_h(h_scr, x_refs, ng_ref, mod_ref):
    ng = ng_ref[...]
    shift = mod_ref[0, 0:1, :]
    scale = mod_ref[0, 1:2, :]
    r0 = 0
    for xr in x_refs:
        rows = xr.shape[1]
        h = _modulated_norm(xr[0], ng, shift, scale)
        h_scr[pl.ds(r0, rows), :] = h.astype(BF16)
        r0 += rows


def _residual(x_ref, mod_ref, acc):
    return x_ref[0] + mod_ref[0, 2:3, :] * acc


def _store_slabs(scr, val):
    for s in range(scr.shape[0]):
        scr[s] = val[:, s * LANES:(s + 1) * LANES]


def _shifted(scr, first_row):
    return jnp.concatenate(
        [scr[s, pl.ds(first_row, TM), :] for s in range(scr.shape[0])], axis=1)


def _const_spec(shape):
    nd = len(shape)
    return pl.BlockSpec(shape, lambda *_: (0,) * nd, pipeline_mode=pl.Buffered(1))


def _halo_specs(d, seq):
    per_tile = TM // HALO
    last = seq // HALO - 1
    before = pl.BlockSpec((1, HALO, d),
                          lambda b, s: (b, jnp.maximum(s * per_tile - 1, 0), 0))
    after = pl.BlockSpec((1, HALO, d),
                         lambda b, s: (b, jnp.minimum((s + 1) * per_tile, last), 0))
    return before, after


def _layer_call(kernel, name, x, mod, ng, weights, scratch, halo=True):
    bsz, seq, d = x.shape
    x_specs = [pl.BlockSpec((1, TM, d), lambda b, s: (b, s, 0))]
    if halo:
        before, after = _halo_specs(d, seq)
        x_specs = [before] + x_specs + [after]
    in_specs = (x_specs
                + [pl.BlockSpec((1, 3, d), lambda b, s: (b, 0, 0)),
                   _const_spec((1, d))]
                + [_const_spec(w.shape) for w in weights])
    return pl.pallas_call(
        kernel,
        out_shape=jax.ShapeDtypeStruct(x.shape, x.dtype),
        grid=(bsz, seq // TM),
        in_specs=in_specs,
        out_specs=pl.BlockSpec((1, TM, d), lambda b, s: (b, s, 0)),
        scratch_shapes=scratch,
        compiler_params=pltpu.CompilerParams(
            dimension_semantics=("arbitrary", "arbitrary"),
            vmem_limit_bytes=VMEM_LIMIT_BYTES),
        name=name,
    )(*([x] * len(x_specs)), mod, ng.reshape(1, d), *weights)


def _split_rows(w, blk):
    e, d = w.shape
    return w.astype(BF16).reshape(e // blk, blk, d)


def _split_vec(v, blk):
    k, e = v.shape
    return v.reshape(k, e // blk, blk).transpose(1, 0, 2)


def _layernorm_stats(s1, s2, e):
    mu = s1 / e
    return mu, lax.rsqrt(s2 / e - mu * mu + EPS)


def _accumulate(acc_scr, j, part):
    if j == 0:
        acc_scr[...] = part
    else:
        acc_scr[...] += part


def _gmlp_kernel(x_ref, mod_ref, ng_ref, w_in_ref, lng_ref, lnb_ref,
                 spw_ref, spb_ref, w_out_ref, o_ref, h_scr, v_scr, mu_scr, rstd_scr, acc_scr):
    nblk = w_out_ref.shape[0]
    e = nblk * CB
    hd = e // GM_HEADS
    _fill_h(h_scr, [x_ref], ng_ref, mod_ref)

    for m0 in range(0, TM, ROW_CHUNK):
        rows = pl.ds(m0, ROW_CHUNK)
        s1 = jnp.zeros((ROW_CHUNK, 1), F32)
        s2 = jnp.zeros((ROW_CHUNK, 1), F32)
        for j in range(nblk):
            v = _gelu_tanh(_dot(h_scr[rows, :], w_in_ref[:, e + j * CB:e + (j + 1) * CB]))
            v_scr[j, rows, :] = v
            s1 = s1 + jnp.sum(v, axis=-1, keepdims=True)
            s2 = s2 + jnp.sum(v * v, axis=-1, keepdims=True)
        mu, rstd = _layernorm_stats(s1, s2, e)
        mu_scr[rows, :] = mu
        rstd_scr[rows, :] = rstd
    mu = mu_scr[...]
    rstd = rstd_scr[...]

    n_chunks = TM // CHUNK
    for j in range(nblk):
        vn = ((v_scr[j] - mu) * rstd * lng_ref[j] + lnb_ref[j]).astype(BF16)
        heads = []
        for hh in range(CB // hd):
            g = j * (CB // hd) + hh
            vh = vn[:, hh * hd:(hh + 1) * hd]
            wide = jnp.concatenate(
                [vh[n * CHUNK:(n + 1) * CHUNK, :] for n in range(n_chunks)], axis=1)
            mixed = _dot(spw_ref[g], wide) + spb_ref[g]
            heads.append(jnp.concatenate(
                [mixed[:, n * hd:(n + 1) * hd] for n in range(n_chunks)], axis=0))
        vs = jnp.concatenate(heads, axis=1)
        hx = h_scr[...]
        u = _gelu_tanh(_dot(hx, w_in_ref[:, j * CB:(j + 1) * CB]))
        z = _dot(hx, w_in_ref[:, 2 * e + j * CB:2 * e + (j + 1) * CB])
        y = (u * vs * _silu(z)).astype(BF16)
        _accumulate(acc_scr, j, _dot(y, w_out_ref[j]))
    o_ref[0] = _residual(x_ref, mod_ref, acc_scr[...])


def _gmlp_layer(x, mod, ng, w_in, ln_g, ln_b, sp_w, sp_b, w_out):
    d = x.shape[-1]
    e = w_out.shape[0]
    nblk = e // CB
    weights = [
        w_in.astype(BF16),
        ln_g.reshape(nblk, 1, CB), ln_b.reshape(nblk, 1, CB),
        sp_w.astype(BF16), sp_b.reshape(GM_HEADS, CHUNK, 1),
        _split_rows(w_out, CB),
    ]
    scratch = [pltpu.VMEM((TM, d), BF16),
               pltpu.VMEM((nblk, TM, CB), F32),
               pltpu.VMEM((TM, 1), F32),
               pltpu.VMEM((TM, 1), F32),
               pltpu.VMEM((TM, d), F32)]
    return _layer_call(_gmlp_kernel, "gmlp_layer", x, mod, ng, weights, scratch, halo=False)


def _depthwise_conv(y_scr, dw_ref, dwb_ref, j, first_row, c_scr):
    for s in range(y_scr.shape[0]):
        lanes = pl.ds(s * LANES, LANES)
        bias = dwb_ref[j, :, lanes]
        for r0 in range(0, TM, CONV_ROWS):
            acc = jnp.broadcast_to(bias, (CONV_ROWS, LANES))
            for k in range(CONV_WIDTH):
                acc = acc + (dw_ref[j, pl.ds(k, 1), lanes]
                             * y_scr[s, pl.ds(first_row + r0 + k, CONV_ROWS), :])
            c_scr[j, pl.ds(r0, CONV_ROWS), lanes] = acc


def _conformer_kernel(xp_ref, x_ref, xn_ref, mod_ref, ng_ref, w_in_ref,
                      dw_ref, dwb_ref, lng_ref, lnb_ref, w_out_ref, o_ref,
                      h_scr, y_scr, c_scr, acc_scr, *, seq):
    nblk = w_out_ref.shape[0]
    e = nblk * CB
    rows = TM + 2 * HALO
    _fill_h(h_scr, [xp_ref, x_ref, xn_ref], ng_ref, mod_ref)
    valid = _rows_valid(rows, HALO, seq, pl.program_id(1))
    pad = CONV_WIDTH // 2

    s1 = jnp.zeros((TM, 1), F32)
    s2 = jnp.zeros((TM, 1), F32)
    for j in range(nblk):
        hx = h_scr[...]
        a = _dot(hx, w_in_ref[:, j * CB:(j + 1) * CB])
        g = _dot(hx, w_in_ref[:, e + j * CB:e + (j + 1) * CB])
        y_buf = y_scr.at[j % 2]
        _store_slabs(y_buf, jnp.where(valid, a * _sigmoid(g), 0.0))
        _depthwise_conv(y_buf, dw_ref, dwb_ref, j, HALO - pad, c_scr)
        c = c_scr[j]
        s1 = s1 + jnp.sum(c, axis=-1, keepdims=True)
        s2 = s2 + jnp.sum(c * c, axis=-1, keepdims=True)
    mu, rstd = _layernorm_stats(s1, s2, e)

    for j in range(nblk):
        cn = (c_scr[j] - mu) * rstd * lng_ref[j] + lnb_ref[j]
        z = _dot(h_scr[pl.ds(HALO, TM), :], w_in_ref[:, 2 * e + j * CB:2 * e + (j + 1) * CB])
        y = (_silu(cn) * _silu(z)).astype(BF16)
        _accumulate(acc_scr, j, _dot(y, w_out_ref[j]))
    o_ref[0] = _residual(x_ref, mod_ref, acc_scr[...])


def _conformer_layer(x, mod, ng, w_in, dw_w, dw_b, ln_g, ln_b, w_out):
    d = x.shape[-1]
    seq = x.shape[1]
    e = w_out.shape[0]
    nblk = e // CB
    weights = [
        w_in.astype(BF16),
        _split_vec(dw_w, CB), dw_b.reshape(nblk, 1, CB),
        ln_g.reshape(nblk, 1, CB), ln_b.reshape(nblk, 1, CB),
        _split_rows(w_out, CB),
    ]
    rows = TM + 2 * HALO
    scratch = [pltpu.VMEM((rows, d), BF16),
               pltpu.VMEM((2, CB // LANES, rows, LANES), F32),
               pltpu.VMEM((nblk, TM, CB), F32),
               pltpu.VMEM((TM, d), F32)]
    kernel = functools.partial(_conformer_kernel, seq=seq)
    return _layer_call(kernel, "conformer_layer", x, mod, ng, weights, scratch)


def _shortconv_kernel(xp_ref, x_ref, xn_ref, mod_ref, ng_ref, w_in_ref, cw_ref, w_out_ref,
                      o_ref, h_scr, q_scr, acc_scr, *, seq):
    nblk = w_out_ref.shape[0]
    e = nblk * CB
    rows = TM + 2 * HALO
    _fill_h(h_scr, [xp_ref, x_ref, xn_ref], ng_ref, mod_ref)
    valid = _rows_valid(rows, HALO, seq, pl.program_id(1))
    pad = SHORT_CONV_WIDTH // 2
    acc_scr[...] = jnp.zeros_like(acc_scr)

    def body(j, carry):
        lo = j * CB
        hx = h_scr[...]
        cg = _dot(hx, _cols(w_in_ref, e + lo, CB))
        v = _dot(hx, _cols(w_in_ref, 2 * e + lo, CB))
        _store_slabs(q_scr, jnp.where(valid, cg * v, 0.0))
        cw = cw_ref[j]
        conv = cw[0:1, :] * _shifted(q_scr, HALO - pad)
        for k in range(1, SHORT_CONV_WIDTH):
            conv = conv + cw[k:k + 1, :] * _shifted(q_scr, HALO - pad + k)
        hm = h_scr[pl.ds(HALO, TM), :]
        bg = _dot(hm, _cols(w_in_ref, lo, CB))
        z = _dot(hm, _cols(w_in_ref, 3 * e + lo, CB))
        y = (bg * conv * _silu(z)).astype(BF16)
        acc_scr[...] += _dot(y, w_out_ref[j])
        return carry

    lax.fori_loop(0, nblk, body, 0)
    o_ref[0] = _residual(x_ref, mod_ref, acc_scr[...])


def _shortconv_layer(x, mod, ng, w_in, conv_w, w_out):
    d = x.shape[-1]
    seq = x.shape[1]
    weights = [
        w_in.astype(BF16),
        _split_vec(conv_w, CB),
        _split_rows(w_out, CB),
    ]
    rows = TM + 2 * HALO
    scratch = [pltpu.VMEM((rows, d), BF16),
               pltpu.VMEM((CB // LANES, rows, LANES), F32),
               pltpu.VMEM((TM, d), F32)]
    kernel = functools.partial(_shortconv_kernel, seq=seq)
    return _layer_call(kernel, "shortconv_layer", x, mod, ng, weights, scratch)


def _pool_kernel(xp_ref, x_ref, xn_ref, mod_ref, ng_ref, w_in_ref, pw_ref, pb_ref,
                 sc_ref, w_out_ref, fg_ref, o_ref, h_scr, p_scr, acc_scr, *, seq):
    ngrp, grp, _ = w_out_ref.shape
    e = ngrp * grp
    rows = TM + 2 * HALO
    _fill_h(h_scr, [xp_ref, x_ref, xn_ref], ng_ref, mod_ref)
    valid = _rows_valid(rows, HALO, seq, pl.program_id(1))
    t = pl.program_id(1) * TM + lax.broadcasted_iota(jnp.int32, (TM, 1), 0)

    for gi, w in enumerate(POOL_WINDOWS):
        lo = jnp.maximum(t - w // 2, 0)
        hi = jnp.minimum(t + w - 1 - w // 2, seq - 1)
        inv_cnt = 1.0 / (hi - lo + 1).astype(F32)
        p = _dot(h_scr[...], w_in_ref[:, gi * grp:(gi + 1) * grp])
        _store_slabs(p_scr, jnp.where(valid, p, 0.0))
        win = _shifted(p_scr, HALO - w // 2)
        for k in range(1, w):
            win = win + _shifted(p_scr, HALO - w // 2 + k)
        dlt = (win * inv_cnt - _shifted(p_scr, HALO)).astype(BF16)
        pooled = (_dot(dlt, pw_ref[gi]) + pb_ref[gi]) * sc_ref[gi]
        z = _dot(h_scr[pl.ds(HALO, TM), :], w_in_ref[:, e + gi * grp:e + (gi + 1) * grp])
        y = (pooled * _silu(z)).astype(BF16)
        _accumulate(acc_scr, gi, _dot(y, w_out_ref[gi]))

    out = _residual(x_ref, mod_ref, acc_scr[...])
    ms = jnp.mean(out * out, axis=-1, keepdims=True)
    o_ref[0] = (out * lax.rsqrt(ms + EPS)) * fg_ref[...]


def _pool_layer(x, mod, ng, w_in, pool_w, pool_b, scale, w_out, final_g):
    d = x.shape[-1]
    seq = x.shape[1]
    e = w_out.shape[0]
    ngrp = len(POOL_WINDOWS)
    grp = e // ngrp
    weights = [
        w_in.astype(BF16),
        pool_w.astype(BF16), pool_b.reshape(ngrp, 1, grp), scale.reshape(ngrp, 1, grp),
        _split_rows(w_out, grp),
        final_g.reshape(1, d),
    ]
    rows = TM + 2 * HALO
    scratch = [pltpu.VMEM((rows, d), BF16),
               pltpu.VMEM((grp // LANES, rows, LANES), F32),
               pltpu.VMEM((TM, d), F32)]
    kernel = functools.partial(_pool_kernel, seq=seq)
    return _layer_call(kernel, "pool_layer", x, mod, ng, weights, scratch)


def kernel(x, c, norm_g, ada_w, ada_b, gm_w_in, gm_ln_g, gm_ln_b, gm_sp_w, gm_sp_b, gm_w_out, cv_w_in, cv_dw_w, cv_dw_b, cv_ln_g, cv_ln_b, cv_w_out, sc_w_in, sc_conv_w, sc_w_out, pl_w_in, pl_w, pl_b, pl_scale, pl_w_out, final_g):
    bsz, seq, d = x.shape
    assert seq % TM == 0 and TM % CHUNK == 0 and TM % HALO == 0
    assert norm_g.shape[0] == 4, "one layer of each mixer kind, final norm fused in the last"
    mod = _ada_mod(c, ada_w, ada_b).reshape(norm_g.shape[0], bsz, 3, d)
    x = _gmlp_layer(x, mod[0], norm_g[0], gm_w_in[0], gm_ln_g[0], gm_ln_b[0],
                    gm_sp_w[0], gm_sp_b[0], gm_w_out[0])
    x = _conformer_layer(x, mod[1], norm_g[1], cv_w_in[0], cv_dw_w[0], cv_dw_b[0],
                         cv_ln_g[0], cv_ln_b[0], cv_w_out[0])
    x = _shortconv_layer(x, mod[2], norm_g[2], sc_w_in[0], sc_conv_w[0], sc_w_out[0])
    return _pool_layer(x, mod[3], norm_g[3], pl_w_in[0], pl_w[0], pl_b[0], pl_scale[0],
                       pl_w_out[0], final_g)
```

```python
import functools
import math

import jax
import jax.numpy as jnp
from jax import lax
from jax.experimental import pallas as pl
from jax.experimental.pallas import tpu as pltpu

EPS = 1e-6
CHUNK = 128
GM_HEADS = 8
CONV_WIDTH = 31
SHORT_CONV_WIDTH = 3
POOL_WINDOWS = (2, 4, 8, 16)

TM = 512
HALO = 16
CB = 512
CONV_ROWS = 64
ROW_CHUNK = 256
LANES = 128
VMEM_LIMIT_BYTES = 56 * 1024 * 1024

BF16 = jnp.bfloat16
F32 = jnp.float32


def _dot(a, b):
    return jnp.dot(a, b, preferred_element_type=F32)


def _sigmoid(x):
    return 0.5 + 0.5 * jnp.tanh(0.5 * x)


def _silu(x):
    h = 0.5 * x
    return h + h * jnp.tanh(h)


def _gelu_tanh(x):
    c = math.sqrt(2.0 / math.pi)
    return 0.5 * x * (1.0 + jnp.tanh(c * (x + 0.044715 * (x * x * x))))


def _modulated_norm(x, gain, shift):
    ms = jnp.mean(x * x, axis=-1, keepdims=True)
    return (x * lax.rsqrt(ms + EPS)) * gain + shift


def _rows_valid(rows, halo, seq, tile_in_seq):
    t = tile_in_seq * TM - halo + lax.broadcasted_iota(jnp.int32, (rows, 1), 0)
    return jnp.logical_and(t >= 0, t < seq)


def _ada_kernel(c_ref, w_ref, b_ref, o_ref):
    c_act = _silu(c_ref[...]).astype(BF16)
    o_ref[0] = _dot(c_act, w_ref[0].astype(BF16)) + b_ref[0]


def _ada_mod(c, ada_w, ada_b):
    depth, d, n = ada_w.shape
    bsz = c.shape[0]
    nb = d
    return pl.pallas_call(
        _ada_kernel,
        out_shape=jax.ShapeDtypeStruct((depth, bsz, n), F32),
        grid=(depth, n // nb),
        in_specs=[
            pl.BlockSpec((bsz, d), lambda i, j: (0, 0)),
            pl.BlockSpec((1, d, nb), lambda i, j: (i, 0, j)),
            pl.BlockSpec((1, 1, nb), lambda i, j: (i, 0, j)),
        ],
        out_specs=pl.BlockSpec((1, bsz, nb), lambda i, j: (i, 0, j)),
        compiler_params=pltpu.CompilerParams(
            dimension_semantics=("arbitrary", "arbitrary")),
        name="ada_mod",
    )(c, ada_w, ada_b.reshape(depth, 1, n))


def _norm_params(ng_ref, mod_ref):
    shift = mod_ref[0, 0:1, :]
    gain = ng_ref[...] * (1.0 + mod_ref[0, 1:2, :])
    return gain, shift


def _fill_h(h_scr, x_refs, ng_ref, mod_ref):
    gain, shift = _norm_params(ng_ref, mod_ref)
    r0 = 0
    for xr in x_refs:
        rows = xr.shape[1]
        h_scr[pl.ds(r0, rows), :] = _modulated_norm(xr[0], gain, shift).astype(BF16)
        r0 += rows


def _residual(x_ref, mod_ref, acc):
    return x_ref[0] + mod_ref[0, 2:3, :] * acc


def _store_slabs(scr, val):
    for s in range(scr.shape[0]):
        scr[s] = val[:, s * LANES:(s + 1) * LANES]


def _shifted(scr, first_row):
    return jnp.concatenate(
        [scr[s, pl.ds(first_row, TM), :] for s in range(scr.shape[0])], axis=1)


def _const_spec(shape):
    nd = len(shape)
    return pl.BlockSpec(shape, lambda *_: (0,) * nd, pipeline_mode=pl.Buffered(1))


def _halo_specs(d, seq):
    per_tile = TM // HALO
    last = seq // HALO - 1
    before = pl.BlockSpec((1, HALO, d),
                          lambda b, s: (b, jnp.maximum(s * per_tile - 1, 0), 0))
    after = pl.BlockSpec((1, HALO, d),
                         lambda b, s: (b, jnp.minimum((s + 1) * per_tile, last), 0))
    return before, after


def _layer_call(kernel, name, x, mod, ng, weights, scratch, halo=True):
    bsz, seq, d = x.shape
    x_specs = [pl.BlockSpec((1, TM, d), lambda b, s: (b, s, 0))]
    if halo:
        before, after = _halo_specs(d, seq)
        x_specs = [before] + x_specs + [after]
    in_specs = (x_specs
                + [pl.BlockSpec((1, 3, d), lambda b, s: (b, 0, 0)),
                   _const_spec((1, d))]
                + [_const_spec(w.shape) for w in weights])
    return pl.pallas_call(
        kernel,
        out_shape=jax.ShapeDtypeStruct(x.shape, x.dtype),
        grid=(bsz, seq // TM),
        in_specs=in_specs,
        out_specs=pl.BlockSpec((1, TM, d), lambda b, s: (b, s, 0)),
        scratch_shapes=scratch,
        compiler_params=pltpu.CompilerParams(
            dimension_semantics=("arbitrary", "arbitrary"),
            vmem_limit_bytes=VMEM_LIMIT_BYTES),
        name=name,
    )(*([x] * len(x_specs)), mod, ng.reshape(1, d), *weights)


def _split_rows(w, blk):
    e, d = w.shape
    return w.astype(BF16).reshape(e // blk, blk, d)


def _split_vec(v, blk):
    k, e = v.shape
    return v.reshape(k, e // blk, blk).transpose(1, 0, 2)


def _layernorm_stats(s1, s2, e):
    mu = s1 / e
    return mu, lax.rsqrt(s2 / e - mu * mu + EPS)


def _accumulate(acc_scr, j, part):
    if j == 0:
        acc_scr[...] = part
    else:
        acc_scr[...] += part


def _gmlp_kernel(x_ref, mod_ref, ng_ref, w_in_ref, lng_ref, lnb_ref,
                 spw_ref, spb_ref, w_out_ref, o_ref, h_scr, v_scr, mu_scr, rstd_scr, acc_scr):
    nblk = w_out_ref.shape[0]
    e = nblk * CB
    hd = e // GM_HEADS
    gain, shift = _norm_params(ng_ref, mod_ref)

    for m0 in range(0, TM, ROW_CHUNK):
        rows = pl.ds(m0, ROW_CHUNK)
        h_scr[rows, :] = _modulated_norm(x_ref[0, rows, :], gain, shift).astype(BF16)
        s1 = jnp.zeros((ROW_CHUNK, 1), F32)
        s2 = jnp.zeros((ROW_CHUNK, 1), F32)
        for j in range(nblk):
            v = _gelu_tanh(_dot(h_scr[rows, :], w_in_ref[:, e + j * CB:e + (j + 1) * CB]))
            v_scr[j, rows, :] = v
            s1 = s1 + jnp.sum(v, axis=-1, keepdims=True)
            s2 = s2 + jnp.sum(v * v, axis=-1, keepdims=True)
        mu, rstd = _layernorm_stats(s1, s2, e)
        mu_scr[rows, :] = mu
        rstd_scr[rows, :] = rstd
    mu = mu_scr[...]
    rstd = rstd_scr[...]

    n_chunks = TM // CHUNK
    for j in range(nblk):
        vn = ((v_scr[j] - mu) * rstd * lng_ref[j] + lnb_ref[j]).astype(BF16)
        heads = []
        for hh in range(CB // hd):
            g = j * (CB // hd) + hh
            vh = vn[:, hh * hd:(hh + 1) * hd]
            wide = jnp.concatenate(
                [vh[n * CHUNK:(n + 1) * CHUNK, :] for n in range(n_chunks)], axis=1)
            mixed = _dot(spw_ref[g], wide) + spb_ref[g]
            heads.append(jnp.concatenate(
                [mixed[:, n * hd:(n + 1) * hd] for n in range(n_chunks)], axis=0))
        vs = jnp.concatenate(heads, axis=1)
        hx = h_scr[...]
        u = _gelu_tanh(_dot(hx, w_in_ref[:, j * CB:(j + 1) * CB]))
        z = _dot(hx, w_in_ref[:, 2 * e + j * CB:2 * e + (j + 1) * CB])
        y = (u * vs * _silu(z)).astype(BF16)
        _accumulate(acc_scr, j, _dot(y, w_out_ref[j]))
    o_ref[0] = _residual(x_ref, mod_ref, acc_scr[...])


def _gmlp_layer(x, mod, ng, w_in, ln_g, ln_b, sp_w, sp_b, w_out):
    d = x.shape[-1]
    e = w_out.shape[0]
    nblk = e // CB
    weights = [
        w_in.astype(BF16),
        ln_g.reshape(nblk, 1, CB), ln_b.reshape(nblk, 1, CB),
        sp_w.astype(BF16), sp_b.reshape(GM_HEADS, CHUNK, 1),
        _split_rows(w_out, CB),
    ]
    scratch = [pltpu.VMEM((TM, d), BF16),
               pltpu.VMEM((nblk, TM, CB), F32),
               pltpu.VMEM((TM, 1), F32),
               pltpu.VMEM((TM, 1), F32),
               pltpu.VMEM((TM, d), F32)]
    return _layer_call(_gmlp_kernel, "gmlp_layer", x, mod, ng, weights, scratch, halo=False)


def _depthwise_conv(y_scr, dw_ref, dwb_ref, j, first_row, c_scr):
    for s in range(y_scr.shape[0]):
        lanes = pl.ds(s * LANES, LANES)
        bias = dwb_ref[j, :, lanes]
        for r0 in range(0, TM, CONV_ROWS):
            acc = jnp.broadcast_to(bias, (CONV_ROWS, LANES))
            for k in range(CONV_WIDTH):
                acc = acc + (dw_ref[j, pl.ds(k, 1), lanes]
                             * y_scr[s, pl.ds(first_row + r0 + k, CONV_ROWS), :])
            c_scr[j, pl.ds(r0, CONV_ROWS), lanes] = acc


def _conformer_kernel(xp_ref, x_ref, xn_ref, mod_ref, ng_ref, w_in_ref,
                      dw_ref, dwb_ref, lng_ref, lnb_ref, w_out_ref, o_ref,
                      h_scr, y_scr, c_scr, acc_scr, *, seq):
    nblk = w_out_ref.shape[0]
    e = nblk * CB
    rows = TM + 2 * HALO
    _fill_h(h_scr, [xp_ref, x_ref, xn_ref], ng_ref, mod_ref)
    valid = _rows_valid(rows, HALO, seq, pl.program_id(1))
    pad = CONV_WIDTH // 2

    s1 = jnp.zeros((TM, 1), F32)
    s2 = jnp.zeros((TM, 1), F32)
    for j in range(nblk):
        hx = h_scr[...]
        a = _dot(hx, w_in_ref[:, j * CB:(j + 1) * CB])
        g = _dot(hx, w_in_ref[:, e + j * CB:e + (j + 1) * CB])
        y_buf = y_scr.at[j % 2]
        _store_slabs(y_buf, jnp.where(valid, a * _sigmoid(g), 0.0))
        _depthwise_conv(y_buf, dw_ref, dwb_ref, j, HALO - pad, c_scr)
        c = c_scr[j]
        s1 = s1 + jnp.sum(c, axis=-1, keepdims=True)
        s2 = s2 + jnp.sum(c * c, axis=-1, keepdims=True)
    mu, rstd = _layernorm_stats(s1, s2, e)

    for j in range(nblk):
        cn = (c_scr[j] - mu) * rstd * lng_ref[j] + lnb_ref[j]
        z = _dot(h_scr[pl.ds(HALO, TM), :], w_in_ref[:, 2 * e + j * CB:2 * e + (j + 1) * CB])
        y = (_silu(cn) * _silu(z)).astype(BF16)
        _accumulate(acc_scr, j, _dot(y, w_out_ref[j]))
    o_ref[0] = _residual(x_ref, mod_ref, acc_scr[...])


def _conformer_layer(x, mod, ng, w_in, dw_w, dw_b, ln_g, ln_b, w_out):
    d = x.shape[-1]
    seq = x.shape[1]
    e = w_out.shape[0]
    nblk = e // CB
    weights = [
        w_in.astype(BF16),
        _split_vec(dw_w, CB), dw_b.reshape(nblk, 1, CB),
        ln_g.reshape(nblk, 1, CB), ln_b.reshape(nblk, 1, CB),
        _split_rows(w_out, CB),
    ]
    rows = TM + 2 * HALO
    scratch = [pltpu.VMEM((rows, d), BF16),
               pltpu.VMEM((2, CB // LANES, rows, LANES), F32),
               pltpu.VMEM((nblk, TM, CB), F32),
               pltpu.VMEM((TM, d), F32)]
    kernel = functools.partial(_conformer_kernel, seq=seq)
    return _layer_call(kernel, "conformer_layer", x, mod, ng, weights, scratch)


def _shortconv_kernel(xp_ref, x_ref, xn_ref, mod_ref, ng_ref, w_in_ref, cw_ref, w_out_ref,
                      o_ref, h_scr, q_scr, acc_scr, *, seq):
    nblk = w_out_ref.shape[0]
    e = nblk * CB
    rows = TM + 2 * HALO
    _fill_h(h_scr, [xp_ref, x_ref, xn_ref], ng_ref, mod_ref)
    valid = _rows_valid(rows, HALO, seq, pl.program_id(1))
    pad = SHORT_CONV_WIDTH // 2
    for j in range(nblk):
        hx = h_scr[...]
        cg = _dot(hx, w_in_ref[:, e + j * CB:e + (j + 1) * CB])
        v = _dot(hx, w_in_ref[:, 2 * e + j * CB:2 * e + (j + 1) * CB])
        q_buf = q_scr.at[j % 2]
        _store_slabs(q_buf, jnp.where(valid, cg * v, 0.0))
        cw = cw_ref[j]
        conv = cw[0:1, :] * _shifted(q_buf, HALO - pad)
        for k in range(1, SHORT_CONV_WIDTH):
            conv = conv + cw[k:k + 1, :] * _shifted(q_buf, HALO - pad + k)
        hm = h_scr[pl.ds(HALO, TM), :]
        bg = _dot(hm, w_in_ref[:, j * CB:(j + 1) * CB])
        z = _dot(hm, w_in_ref[:, 3 * e + j * CB:3 * e + (j + 1) * CB])
        y = (bg * conv * _silu(z)).astype(BF16)
        _accumulate(acc_scr, j, _dot(y, w_out_ref[j]))
    o_ref[0] = _residual(x_ref, mod_ref, acc_scr[...])


def _shortconv_layer(x, mod, ng, w_in, conv_w, w_out):
    d = x.shape[-1]
    seq = x.shape[1]
    weights = [
        w_in.astype(BF16),
        _split_vec(conv_w, CB),
        _split_rows(w_out, CB),
    ]
    rows = TM + 2 * HALO
    scratch = [pltpu.VMEM((rows, d), BF16),
               pltpu.VMEM((2, CB // LANES, rows, LANES), F32),
               pltpu.VMEM((TM, d), F32)]
    kernel = functools.partial(_shortconv_kernel, seq=seq)
    return _layer_call(kernel, "shortconv_layer", x, mod, ng, weights, scratch)


def _pool_kernel(xp_ref, x_ref, xn_ref, mod_ref, ng_ref, w_in_ref, pw_ref, pb_ref,
                 sc_ref, w_out_ref, fg_ref, o_ref, h_scr, p_scr, acc_scr, *, seq):
    ngrp, grp, _ = w_out_ref.shape
    e = ngrp * grp
    rows = TM + 2 * HALO
    _fill_h(h_scr, [xp_ref, x_ref, xn_ref], ng_ref, mod_ref)
    valid = _rows_valid(rows, HALO, seq, pl.program_id(1))
    t = pl.program_id(1) * TM + lax.broadcasted_iota(jnp.int32, (TM, 1), 0)

    for gi, w in enumerate(POOL_WINDOWS):
        lo = jnp.maximum(t - w // 2, 0)
        hi = jnp.minimum(t + w - 1 - w // 2, seq - 1)
        inv_cnt = 1.0 / (hi - lo + 1).astype(F32)
        p = _dot(h_scr[...], w_in_ref[:, gi * grp:(gi + 1) * grp])
        p_buf = p_scr.at[gi % 2]
        _store_slabs(p_buf, jnp.where(valid, p, 0.0))
        win = _shifted(p_buf, HALO - w // 2)
        for k in range(1, w):
            win = win + _shifted(p_buf, HALO - w // 2 + k)
        dlt = (win * inv_cnt - _shifted(p_buf, HALO)).astype(BF16)
        pooled = (_dot(dlt, pw_ref[gi]) + pb_ref[gi]) * sc_ref[gi]
        z = _dot(h_scr[pl.ds(HALO, TM), :], w_in_ref[:, e + gi * grp:e + (gi + 1) * grp])
        y = (pooled * _silu(z)).astype(BF16)
        _accumulate(acc_scr, gi, _dot(y, w_out_ref[gi]))

    out = _residual(x_ref, mod_ref, acc_scr[...])
    ms = jnp.mean(out * out, axis=-1, keepdims=True)
    o_ref[0] = (out * lax.rsqrt(ms + EPS)) * fg_ref[...]


def _pool_layer(x, mod, ng, w_in, pool_w, pool_b, scale, w_out, final_g):
    d = x.shape[-1]
    seq = x.shape[1]
    e = w_out.shape[0]
    ngrp = len(POOL_WINDOWS)
    grp = e // ngrp
    weights = [
        w_in.astype(BF16),
        pool_w.astype(BF16), pool_b.reshape(ngrp, 1, grp), scale.reshape(ngrp, 1, grp),
        _split_rows(w_out, grp),
        final_g.reshape(1, d),
    ]
    rows = TM + 2 * HALO
    scratch = [pltpu.VMEM((rows, d), BF16),
               pltpu.VMEM((2, grp // LANES, rows, LANES), F32),
               pltpu.VMEM((TM, d), F32)]
    kernel = functools.partial(_pool_kernel, seq=seq)
    return _layer_call(kernel, "pool_layer", x, mod, ng, weights, scratch)


def kernel(x, c, norm_g, ada_w, ada_b, gm_w_in, gm_ln_g, gm_ln_b, gm_sp_w, gm_sp_b, gm_w_out, cv_w_in, cv_dw_w, cv_dw_b, cv_ln_g, cv_ln_b, cv_w_out, sc_w_in, sc_conv_w, sc_w_out, pl_w_in, pl_w, pl_b, pl_scale, pl_w_out, final_g):
    bsz, seq, d = x.shape
    assert seq % TM == 0 and TM % CHUNK == 0 and TM % HALO == 0
    assert norm_g.shape[0] == 4, "one layer of each mixer kind, final norm fused in the last"
    mod = _ada_mod(c, ada_w, ada_b).reshape(norm_g.shape[0], bsz, 3, d)
    x = _gmlp_layer(x, mod[0], norm_g[0], gm_w_in[0], gm_ln_g[0], gm_ln_b[0],
                    gm_sp_w[0], gm_sp_b[0], gm_w_out[0])
    x = _conformer_layer(x, mod[1], norm_g[1], cv_w_in[0], cv_dw_w[0], cv_dw_b[0],
                         cv_ln_g[0], cv_ln_b[0], cv_w_out[0])
    x = _shortconv_layer(x, mod[2], norm_g[2], sc_w_in[0], sc_conv_w[0], sc_w_out[0])
    return _pool_layer(x, mod[3], norm_g[3], pl_w_in[0], pl_w[0], pl_b[0], pl_scale[0],
                       pl_w_out[0], final_g)
```

```python
import functools
import math

import jax
import jax.numpy as jnp
from jax import lax
from jax.experimental import pallas as pl
from jax.experimental.pallas import tpu as pltpu

EPS = 1e-6
CHUNK = 128
GM_HEADS = 8
CONV_WIDTH = 31
SHORT_CONV_WIDTH = 3
POOL_WINDOWS = (2, 4, 8, 16)

TM = 512
HALO = 16
CB = 1024
CONV_ROWS = 64
LANES = 128
VMEM_LIMIT_BYTES = 56 * 1024 * 1024

BF16 = jnp.bfloat16
F32 = jnp.float32


def _dot(a, b):
    return jnp.dot(a, b, preferred_element_type=F32)


def _sigmoid(x):
    return 0.5 + 0.5 * jnp.tanh(0.5 * x)


def _silu(x):
    h = 0.5 * x
    return h + h * jnp.tanh(h)


def _gelu_tanh(x):
    c = math.sqrt(2.0 / math.pi)
    return 0.5 * x * (1.0 + jnp.tanh(c * (x + 0.044715 * (x * x * x))))


def _modulated_norm(x, gain, shift):
    ms = jnp.mean(x * x, axis=-1, keepdims=True)
    return (x * lax.rsqrt(ms + EPS)) * gain + shift


def _rows_valid(rows, halo, seq, tile_in_seq):
    t = tile_in_seq * TM - halo + lax.broadcasted_iota(jnp.int32, (rows, 1), 0)
    return jnp.logical_and(t >= 0, t < seq)


def _ada_kernel(c_ref, w_ref, b_ref, o_ref):
    c_act = _silu(c_ref[...]).astype(BF16)
    o_ref[0] = _dot(c_act, w_ref[0].astype(BF16)) + b_ref[0]


def _ada_mod(c, ada_w, ada_b):
    depth, d, n = ada_w.shape
    bsz = c.shape[0]
    nb = d
    return pl.pallas_call(
        _ada_kernel,
        out_shape=jax.ShapeDtypeStruct((depth, bsz, n), F32),
        grid=(depth, n // nb),
        in_specs=[
            pl.BlockSpec((bsz, d), lambda i, j: (0, 0)),
            pl.BlockSpec((1, d, nb), lambda i, j: (i, 0, j)),
            pl.BlockSpec((1, 1, nb), lambda i, j: (i, 0, j)),
        ],
        out_specs=pl.BlockSpec((1, bsz, nb), lambda i, j: (i, 0, j)),
        compiler_params=pltpu.CompilerParams(
            dimension_semantics=("arbitrary", "arbitrary")),
        name="ada_mod",
    )(c, ada_w, ada_b.reshape(depth, 1, n))


def _norm_params(ng_ref, mod_ref):
    shift = mod_ref[0, 0:1, :]
    gain = ng_ref[...] * (1.0 + mod_ref[0, 1:2, :])
    return gain, shift


def _fill_h(h_scr, x_refs, ng_ref, mod_ref):
    gain, shift = _norm_params(ng_ref, mod_ref)
    r0 = 0
    for xr in x_refs:
        rows = xr.shape[1]
        h_scr[pl.ds(r0, rows), :] = _modulated_norm(xr[0], gain, shift).astype(BF16)
        r0 += rows


def _residual(x_ref, mod_ref, acc):
    return x_ref[0] + mod_ref[0, 2:3, :] * acc


def _store_slabs(scr, val):
    for s in range(scr.shape[0]):
        scr[s] = val[:, s * LANES:(s + 1) * LANES]


def _shifted(scr, first_row):
    return jnp.concatenate(
        [scr[s, pl.ds(first_row, TM), :] for s in range(scr.shape[0])], axis=1)


def _const_spec(shape):
    nd = len(shape)
    return pl.BlockSpec(shape, lambda *_: (0,) * nd, pipeline_mode=pl.Buffered(1))


def _halo_specs(d, seq):
    per_tile = TM // HALO
    last = seq // HALO - 1
    before = pl.BlockSpec((1, HALO, d),
                          lambda b, s: (b, jnp.maximum(s * per_tile - 1, 0), 0))
    after = pl.BlockSpec((1, HALO, d),
                         lambda b, s: (b, jnp.minimum((s + 1) * per_tile, last), 0))
    return before, after


def _layer_call(kernel, name, x, mod, ng, weights, scratch, halo=True):
    bsz, seq, d = x.shape
    x_specs = [pl.BlockSpec((1, TM, d), lambda b, s: (b, s, 0))]
    if halo:
        before, after = _halo_specs(d, seq)
        x_specs = [before] + x_specs + [after]
    in_specs = (x_specs
                + [pl.BlockSpec((1, 3, d), lambda b, s: (b, 0, 0)),
                   _const_spec((1, d))]
                + [_const_spec(w.shape) for w in weights])
    return pl.pallas_call(
        kernel,
        out_shape=jax.ShapeDtypeStruct(x.shape, x.dtype),
        grid=(bsz, seq // TM),
        in_specs=in_specs,
        out_specs=pl.BlockSpec((1, TM, d), lambda b, s: (b, s, 0)),
        scratch_shapes=scratch,
        compiler_params=pltpu.CompilerParams(
            dimension_semantics=("arbitrary", "arbitrary"),
            vmem_limit_bytes=VMEM_LIMIT_BYTES),
        name=name,
    )(*([x] * len(x_specs)), mod, ng.reshape(1, d), *weights)


def _split_rows(w, blk):
    e, d = w.shape
    return w.astype(BF16).reshape(e // blk, blk, d)


def _split_vec(v, blk):
    k, e = v.shape
    return v.reshape(k, e // blk, blk).transpose(1, 0, 2)


def _layernorm_stats(s1, s2, e):
    mu = s1 / e
    return mu, lax.rsqrt(s2 / e - mu * mu + EPS)


def _accumulate(acc_scr, j, part):
    if j == 0:
        acc_scr[...] = part
    else:
        acc_scr[...] += part


def _gmlp_kernel(x_ref, mod_ref, ng_ref, w_in_ref, lng_ref, lnb_ref,
                 spw_ref, spb_ref, w_out_ref, o_ref, h_scr, v_scr, g_scr):
    e = w_out_ref.shape[0]
    hd = e // GM_HEADS
    _fill_h(h_scr, [x_ref], ng_ref, mod_ref)

    hx = h_scr[...]
    v = _gelu_tanh(_dot(hx, w_in_ref[:, e:2 * e]))
    v_scr[...] = v
    mu, rstd = _layernorm_stats(jnp.sum(v, axis=-1, keepdims=True),
                                jnp.sum(v * v, axis=-1, keepdims=True), e)
    u = _gelu_tanh(_dot(hx, w_in_ref[:, 0:e]))
    z = _dot(hx, w_in_ref[:, 2 * e:3 * e])
    g_scr[...] = u * _silu(z)

    vn = ((v_scr[...] - mu) * rstd * lng_ref[...] + lnb_ref[...]).astype(BF16)
    n_chunks = TM // CHUNK
    heads = []
    for g in range(GM_HEADS):
        vh = vn[:, g * hd:(g + 1) * hd]
        wide = jnp.concatenate(
            [vh[n * CHUNK:(n + 1) * CHUNK, :] for n in range(n_chunks)], axis=1)
        mixed = _dot(spw_ref[g], wide) + spb_ref[g]
        heads.append(jnp.concatenate(
            [mixed[:, n * hd:(n + 1) * hd] for n in range(n_chunks)], axis=0))
    vs = jnp.concatenate(heads, axis=1)
    y = (g_scr[...] * vs).astype(BF16)
    o_ref[0] = _residual(x_ref, mod_ref, _dot(y, w_out_ref[...]))


def _gmlp_layer(x, mod, ng, w_in, ln_g, ln_b, sp_w, sp_b, w_out):
    d = x.shape[-1]
    e = w_out.shape[0]
    weights = [
        w_in.astype(BF16),
        ln_g.reshape(1, e), ln_b.reshape(1, e),
        sp_w.astype(BF16), sp_b.reshape(GM_HEADS, CHUNK, 1),
        w_out.astype(BF16),
    ]
    scratch = [pltpu.VMEM((TM, d), BF16),
               pltpu.VMEM((TM, e), F32),
               pltpu.VMEM((TM, e), F32)]
    return _layer_call(_gmlp_kernel, "gmlp_layer", x, mod, ng, weights, scratch, halo=False)


def _depthwise_conv(y_scr, dw_ref, dwb_ref, first_row, c_scr):
    for s in range(y_scr.shape[0]):
        lanes = pl.ds(s * LANES, LANES)
        bias = dwb_ref[:, lanes]
        for r0 in range(0, TM, CONV_ROWS):
            acc = jnp.broadcast_to(bias, (CONV_ROWS, LANES))
            for k in range(CONV_WIDTH):
                acc = acc + (dw_ref[pl.ds(k, 1), lanes]
                             * y_scr[s, pl.ds(first_row + r0 + k, CONV_ROWS), :])
            c_scr[pl.ds(r0, CONV_ROWS), lanes] = acc


def _conformer_kernel(xp_ref, x_ref, xn_ref, mod_ref, ng_ref, w_in_ref,
                      dw_ref, dwb_ref, lng_ref, lnb_ref, w_out_ref, o_ref,
                      h_scr, y_scr, c_scr, *, seq):
    e = w_out_ref.shape[0]
    rows = TM + 2 * HALO
    _fill_h(h_scr, [xp_ref, x_ref, xn_ref], ng_ref, mod_ref)
    valid = _rows_valid(rows, HALO, seq, pl.program_id(1))
    pad = CONV_WIDTH // 2

    hx = h_scr[...]
    a = _dot(hx, w_in_ref[:, 0:e])
    g = _dot(hx, w_in_ref[:, e:2 * e])
    _store_slabs(y_scr, jnp.where(valid, a * _sigmoid(g), 0.0))
    _depthwise_conv(y_scr, dw_ref, dwb_ref, HALO - pad, c_scr)
    c = c_scr[...]
    mu, rstd = _layernorm_stats(jnp.sum(c, axis=-1, keepdims=True),
                                jnp.sum(c * c, axis=-1, keepdims=True), e)
    cn = (c - mu) * rstd * lng_ref[...] + lnb_ref[...]
    z = _dot(h_scr[pl.ds(HALO, TM), :], w_in_ref[:, 2 * e:3 * e])
    y = (_silu(cn) * _silu(z)).astype(BF16)
    o_ref[0] = _residual(x_ref, mod_ref, _dot(y, w_out_ref[...]))


def _conformer_layer(x, mod, ng, w_in, dw_w, dw_b, ln_g, ln_b, w_out):
    d = x.shape[-1]
    seq = x.shape[1]
    e = w_out.shape[0]
    weights = [
        w_in.astype(BF16),
        dw_w, dw_b.reshape(1, e), ln_g.reshape(1, e), ln_b.reshape(1, e),
        w_out.astype(BF16),
    ]
    rows = TM + 2 * HALO
    scratch = [pltpu.VMEM((rows, d), BF16),
               pltpu.VMEM((e // LANES, rows, LANES), F32),
               pltpu.VMEM((TM, e), F32)]
    kernel = functools.partial(_conformer_kernel, seq=seq)
    return _layer_call(kernel, "conformer_layer", x, mod, ng, weights, scratch)


def _shortconv_kernel(xp_ref, x_ref, xn_ref, mod_ref, ng_ref, w_in_ref, cw_ref, w_out_ref,
                      o_ref, h_scr, q_scr, acc_scr, *, seq):
    nblk = w_out_ref.shape[0]
    e = nblk * CB
    rows = TM + 2 * HALO
    _fill_h(h_scr, [xp_ref, x_ref, xn_ref], ng_ref, mod_ref)
    valid = _rows_valid(rows, HALO, seq, pl.program_id(1))
    pad = SHORT_CONV_WIDTH // 2
    for j in range(nblk):
        hx = h_scr[...]
        cg = _dot(hx, w_in_ref[:, e + j * CB:e + (j + 1) * CB])
        v = _dot(hx, w_in_ref[:, 2 * e + j * CB:2 * e + (j + 1) * CB])
        q_buf = q_scr.at[j % 2]
        _store_slabs(q_buf, jnp.where(valid, cg * v, 0.0))
        cw = cw_ref[j]
        conv = cw[0:1, :] * _shifted(q_buf, HALO - pad)
        for k in range(1, SHORT_CONV_WIDTH):
            conv = conv + cw[k:k + 1, :] * _shifted(q_buf, HALO - pad + k)
        hm = h_scr[pl.ds(HALO, TM), :]
        bg = _dot(hm, w_in_ref[:, j * CB:(j + 1) * CB])
        z = _dot(hm, w_in_ref[:, 3 * e + j * CB:3 * e + (j + 1) * CB])
        y = (bg * conv * _silu(z)).astype(BF16)
        _accumulate(acc_scr, j, _dot(y, w_out_ref[j]))
    o_ref[0] = _residual(x_ref, mod_ref, acc_scr[...])


def _shortconv_layer(x, mod, ng, w_in, conv_w, w_out):
    d = x.shape[-1]
    seq = x.shape[1]
    weights = [
        w_in.astype(BF16),
        _split_vec(conv_w, CB),
        _split_rows(w_out, CB),
    ]
    rows = TM + 2 * HALO
    scratch = [pltpu.VMEM((rows, d), BF16),
               pltpu.VMEM((2, CB // LANES, rows, LANES), F32),
               pltpu.VMEM((TM, d), F32)]
    kernel = functools.partial(_shortconv_kernel, seq=seq)
    return _layer_call(kernel, "shortconv_layer", x, mod, ng, weights, scratch)


def _pool_kernel(xp_ref, x_ref, xn_ref, mod_ref, ng_ref, w_in_ref, pw_ref, pb_ref,
                 sc_ref, w_out_ref, fg_ref, o_ref, h_scr, p_scr, *, seq):
    ngrp, grp, _ = pw_ref.shape
    e = ngrp * grp
    rows = TM + 2 * HALO
    _fill_h(h_scr, [xp_ref, x_ref, xn_ref], ng_ref, mod_ref)
    valid = _rows_valid(rows, HALO, seq, pl.program_id(1))
    t = pl.program_id(1) * TM + lax.broadcasted_iota(jnp.int32, (TM, 1), 0)

    sz = _silu(_dot(h_scr[pl.ds(HALO, TM), :], w_in_ref[:, e:2 * e]))
    _store_slabs(p_scr, jnp.where(valid, _dot(h_scr[...], w_in_ref[:, 0:e]), 0.0))
    per = grp // LANES
    parts = []
    for gi, w in enumerate(POOL_WINDOWS):
        lo = jnp.maximum(t - w // 2, 0)
        hi = jnp.minimum(t + w - 1 - w // 2, seq - 1)
        inv_cnt = 1.0 / (hi - lo + 1).astype(F32)
        p_buf = p_scr.at[pl.ds(gi * per, per)]
        win = _shifted(p_buf, HALO - w // 2)
        for k in range(1, w):
            win = win + _shifted(p_buf, HALO - w // 2 + k)
        dlt = (win * inv_cnt - _shifted(p_buf, HALO)).astype(BF16)
        pooled = (_dot(dlt, pw_ref[gi]) + pb_ref[gi]) * sc_ref[gi]
        parts.append((pooled * sz[:, gi * grp:(gi + 1) * grp]).astype(BF16))
    y = jnp.concatenate(parts, axis=1)
    out = _residual(x_ref, mod_ref, _dot(y, w_out_ref[...]))
    ms = jnp.mean(out * out, axis=-1, keepdims=True)
    o_ref[0] = (out * lax.rsqrt(ms + EPS)) * fg_ref[...]


def _pool_layer(x, mod, ng, w_in, pool_w, pool_b, scale, w_out, final_g):
    d = x.shape[-1]
    seq = x.shape[1]
    e = w_out.shape[0]
    ngrp = len(POOL_WINDOWS)
    grp = e // ngrp
    weights = [
        w_in.astype(BF16),
        pool_w.astype(BF16), pool_b.reshape(ngrp, 1, grp), scale.reshape(ngrp, 1, grp),
        w_out.astype(BF16),
        final_g.reshape(1, d),
    ]
    rows = TM + 2 * HALO
    scratch = [pltpu.VMEM((rows, d), BF16),
               pltpu.VMEM((e // LANES, rows, LANES), F32)]
    kernel = functools.partial(_pool_kernel, seq=seq)
    return _layer_call(kernel, "pool_layer", x, mod, ng, weights, scratch)


def kernel(x, c, norm_g, ada_w, ada_b, gm_w_in, gm_ln_g, gm_ln_b, gm_sp_w, gm_sp_b, gm_w_out, cv_w_in, cv_dw_w, cv_dw_b, cv_ln_g, cv_ln_b, cv_w_out, sc_w_in, sc_conv_w, sc_w_out, pl_w_in, pl_w, pl_b, pl_scale, pl_w_out, final_g):
    bsz, seq, d = x.shape
    assert seq % TM == 0 and TM % CHUNK == 0 and TM % HALO == 0
    assert norm_g.shape[0] == 4, "one layer of each mixer kind, final norm fused in the last"
    mod = _ada_mod(c, ada_w, ada_b).reshape(norm_g.shape[0], bsz, 3, d)
    x = _gmlp_layer(x, mod[0], norm_g[0], gm_w_in[0], gm_ln_g[0], gm_ln_b[0],
                    gm_sp_w[0], gm_sp_b[0], gm_w_out[0])
    x = _conformer_layer(x, mod[1], norm_g[1], cv_w_in[0], cv_dw_w[0], cv_dw_b[0],
                         cv_ln_g[0], cv_ln_b[0], cv_w_out[0])
    x = _shortconv_layer(x, mod[2], norm_g[2], sc_w_in[0], sc_conv_w[0], sc_w_out[0])
    return _pool_layer(x, mod[3], norm_g[3], pl_w_in[0], pl_w[0], pl_b[0], pl_scale[0],
                       pl_w_out[0], final_g)
```

```python
import functools
import math

import jax
import jax.numpy as jnp
from jax import lax
from jax.experimental import pallas as pl
from jax.experimental.pallas import tpu as pltpu

EPS = 1e-6
CHUNK = 128
GM_HEADS = 8
CONV_WIDTH = 31
SHORT_CONV_WIDTH = 3
POOL_WINDOWS = (2, 4, 8, 16)

TM = 512
HALO = 16
CB = 1024
CONV_ROWS = 64
LANES = 128
VMEM_LIMIT_BYTES = 56 * 1024 * 1024

BF16 = jnp.bfloat16
F32 = jnp.float32


def _dot(a, b):
    return jnp.dot(a, b, preferred_element_type=F32)


def _sigmoid(x):
    return 0.5 + 0.5 * jnp.tanh(0.5 * x)


def _silu(x):
    h = 0.5 * x
    return h + h * jnp.tanh(h)


def _gelu_tanh(x):
    c = math.sqrt(2.0 / math.pi)
    return 0.5 * x * (1.0 + jnp.tanh(c * (x + 0.044715 * (x * x * x))))


def _modulated_norm(x, gain, shift):
    ms = jnp.mean(x * x, axis=-1, keepdims=True)
    return (x * lax.rsqrt(ms + EPS)) * gain + shift


def _zero_outside_sequence(val, seq, tile_in_seq):
    def masked(r0):
        t = tile_in_seq * TM - HALO + r0 + lax.broadcasted_iota(jnp.int32, (HALO, 1), 0)
        inside = jnp.logical_and(t >= 0, t < seq)
        return jnp.where(inside, val[r0:r0 + HALO], 0.0)

    return jnp.concatenate([masked(0), val[HALO:HALO + TM], masked(HALO + TM)], axis=0)


def _ada_kernel(c_ref, w_ref, b_ref, o_ref):
    c_act = _silu(c_ref[...]).astype(BF16)
    o_ref[0] = _dot(c_act, w_ref[0].astype(BF16)) + b_ref[0]


def _ada_mod(c, ada_w, ada_b):
    depth, d, n = ada_w.shape
    bsz = c.shape[0]
    nb = d
    return pl.pallas_call(
        _ada_kernel,
        out_shape=jax.ShapeDtypeStruct((depth, bsz, n), F32),
        grid=(depth, n // nb),
        in_specs=[
            pl.BlockSpec((bsz, d), lambda i, j: (0, 0)),
            pl.BlockSpec((1, d, nb), lambda i, j: (i, 0, j)),
            pl.BlockSpec((1, 1, nb), lambda i, j: (i, 0, j)),
        ],
        out_specs=pl.BlockSpec((1, bsz, nb), lambda i, j: (i, 0, j)),
        compiler_params=pltpu.CompilerParams(
            dimension_semantics=("arbitrary", "arbitrary")),
        name="ada_mod",
    )(c, ada_w, ada_b.reshape(depth, 1, n))


def _norm_params(ng_ref, mod_ref):
    shift = mod_ref[0, 0:1, :]
    gain = ng_ref[...] * (1.0 + mod_ref[0, 1:2, :])
    return gain, shift


def _fill_h(h_scr, x_refs, ng_ref, mod_ref):
    gain, shift = _norm_params(ng_ref, mod_ref)
    r0 = 0
    for xr in x_refs:
        rows = xr.shape[1]
        h_scr[pl.ds(r0, rows), :] = _modulated_norm(xr[0], gain, shift).astype(BF16)
        r0 += rows


def _residual(x_ref, mod_ref, acc):
    return x_ref[0] + mod_ref[0, 2:3, :] * acc


def _store_slabs(scr, val):
    for s in range(scr.shape[0]):
        scr[s] = val[:, s * LANES:(s + 1) * LANES]


def _shifted(scr, first_row, nrows=TM):
    return jnp.concatenate(
        [scr[s, pl.ds(first_row, nrows), :] for s in range(scr.shape[0])], axis=1)


def _window_sum(scr, first_row, w):
    half = 8
    if w == 2 * half:
        part = _shifted(scr, first_row, TM + half)
        for k in range(1, half):
            part = part + _shifted(scr, first_row + k, TM + half)
        return part[0:TM] + part[half:TM + half]
    win = _shifted(scr, first_row)
    for k in range(1, w):
        win = win + _shifted(scr, first_row + k)
    return win


def _const_spec(shape):
    nd = len(shape)
    return pl.BlockSpec(shape, lambda *_: (0,) * nd, pipeline_mode=pl.Buffered(1))


def _halo_specs(d, seq):
    per_tile = TM // HALO
    last = seq // HALO - 1
    before = pl.BlockSpec((1, HALO, d),
                          lambda b, s: (b, jnp.maximum(s * per_tile - 1, 0), 0))
    after = pl.BlockSpec((1, HALO, d),
                         lambda b, s: (b, jnp.minimum((s + 1) * per_tile, last), 0))
    return before, after


def _layer_call(kernel, name, x, mod, ng, weights, scratch, halo=True):
    bsz, seq, d = x.shape
    x_specs = [pl.BlockSpec((1, TM, d), lambda b, s: (b, s, 0))]
    if halo:
        before, after = _halo_specs(d, seq)
        x_specs = [before] + x_specs + [after]
    in_specs = (x_specs
                + [pl.BlockSpec((1, 3, d), lambda b, s: (b, 0, 0)),
                   _const_spec((1, d))]
                + [_const_spec(w.shape) for w in weights])
    return pl.pallas_call(
        kernel,
        out_shape=jax.ShapeDtypeStruct(x.shape, x.dtype),
        grid=(bsz, seq // TM),
        in_specs=in_specs,
        out_specs=pl.BlockSpec((1, TM, d), lambda b, s: (b, s, 0)),
        scratch_shapes=scratch,
        compiler_params=pltpu.CompilerParams(
            dimension_semantics=("arbitrary", "arbitrary"),
            vmem_limit_bytes=VMEM_LIMIT_BYTES),
        name=name,
    )(*([x] * len(x_specs)), mod, ng.reshape(1, d), *weights)


def _split_rows(w, blk):
    e, d = w.shape
    return w.astype(BF16).reshape(e // blk, blk, d)


def _split_vec(v, blk):
    k, e = v.shape
    return v.reshape(k, e // blk, blk).transpose(1, 0, 2)


def _layernorm_stats(s1, s2, e):
    mu = s1 / e
    return mu, lax.rsqrt(s2 / e - mu * mu + EPS)


def _accumulate(acc_scr, j, part):
    if j == 0:
        acc_scr[...] = part
    else:
        acc_scr[...] += part


def _gmlp_kernel(x_ref, mod_ref, ng_ref, w_in_ref, lng_ref, lnb_ref,
                 spw_ref, spb_ref, w_out_ref, o_ref, h_scr, v_scr, g_scr):
    e = w_out_ref.shape[0]
    hd = e // GM_HEADS
    _fill_h(h_scr, [x_ref], ng_ref, mod_ref)

    hx = h_scr[...]
    v = _gelu_tanh(_dot(hx, w_in_ref[:, e:2 * e]))
    v_scr[...] = v
    mu, rstd = _layernorm_stats(jnp.sum(v, axis=-1, keepdims=True),
                                jnp.sum(v * v, axis=-1, keepdims=True), e)
    u = _gelu_tanh(_dot(hx, w_in_ref[:, 0:e]))
    z = _dot(hx, w_in_ref[:, 2 * e:3 * e])
    g_scr[...] = u * _silu(z)

    vn = ((v_scr[...] - mu) * rstd * lng_ref[...] + lnb_ref[...]).astype(BF16)
    n_chunks = TM // CHUNK
    heads = []
    for g in range(GM_HEADS):
        vh = vn[:, g * hd:(g + 1) * hd]
        wide = jnp.concatenate(
            [vh[n * CHUNK:(n + 1) * CHUNK, :] for n in range(n_chunks)], axis=1)
        mixed = _dot(spw_ref[g], wide) + spb_ref[g]
        heads.append(jnp.concatenate(
            [mixed[:, n * hd:(n + 1) * hd] for n in range(n_chunks)], axis=0))
    vs = jnp.concatenate(heads, axis=1)
    y = (g_scr[...] * vs).astype(BF16)
    o_ref[0] = _residual(x_ref, mod_ref, _dot(y, w_out_ref[...]))


def _gmlp_layer(x, mod, ng, w_in, ln_g, ln_b, sp_w, sp_b, w_out):
    d = x.shape[-1]
    e = w_out.shape[0]
    weights = [
        w_in.astype(BF16),
        ln_g.reshape(1, e), ln_b.reshape(1, e),
        sp_w.astype(BF16), sp_b.reshape(GM_HEADS, CHUNK, 1),
        w_out.astype(BF16),
    ]
    scratch = [pltpu.VMEM((TM, d), BF16),
               pltpu.VMEM((TM, e), F32),
               pltpu.VMEM((TM, e), F32)]
    return _layer_call(_gmlp_kernel, "gmlp_layer", x, mod, ng, weights, scratch, halo=False)


def _depthwise_conv(y_scr, dw_ref, dwb_ref, first_row, c_scr):
    for s in range(y_scr.shape[0]):
        lanes = pl.ds(s * LANES, LANES)
        bias = dwb_ref[:, lanes]
        for r0 in range(0, TM, CONV_ROWS):
            acc = jnp.broadcast_to(bias, (CONV_ROWS, LANES))
            for k in range(CONV_WIDTH):
                acc = acc + (dw_ref[pl.ds(k, 1), lanes]
                             * y_scr[s, pl.ds(first_row + r0 + k, CONV_ROWS), :])
            c_scr[pl.ds(r0, CONV_ROWS), lanes] = acc


def _conformer_kernel(xp_ref, x_ref, xn_ref, mod_ref, ng_ref, w_in_ref,
                      dw_ref, dwb_ref, lng_ref, lnb_ref, w_out_ref, o_ref,
                      h_scr, y_scr, c_scr, *, seq):
    e = w_out_ref.shape[0]
    _fill_h(h_scr, [xp_ref, x_ref, xn_ref], ng_ref, mod_ref)
    pad = CONV_WIDTH // 2

    hx = h_scr[...]
    a = _dot(hx, w_in_ref[:, 0:e])
    g = _dot(hx, w_in_ref[:, e:2 * e])
    _store_slabs(y_scr, _zero_outside_sequence(a * _sigmoid(g), seq, pl.program_id(1)))
    _depthwise_conv(y_scr, dw_ref, dwb_ref, HALO - pad, c_scr)
    c = c_scr[...]
    mu, rstd = _layernorm_stats(jnp.sum(c, axis=-1, keepdims=True),
                                jnp.sum(c * c, axis=-1, keepdims=True), e)
    cn = (c - mu) * rstd * lng_ref[...] + lnb_ref[...]
    z = _dot(h_scr[pl.ds(HALO, TM), :], w_in_ref[:, 2 * e:3 * e])
    y = (_silu(cn) * _silu(z)).astype(BF16)
    o_ref[0] = _residual(x_ref, mod_ref, _dot(y, w_out_ref[...]))


def _conformer_layer(x, mod, ng, w_in, dw_w, dw_b, ln_g, ln_b, w_out):
    d = x.shape[-1]
    seq = x.shape[1]
    e = w_out.shape[0]
    weights = [
        w_in.astype(BF16),
        dw_w, dw_b.reshape(1, e), ln_g.reshape(1, e), ln_b.reshape(1, e),
        w_out.astype(BF16),
    ]
    rows = TM + 2 * HALO
    scratch = [pltpu.VMEM((rows, d), BF16),
               pltpu.VMEM((e // LANES, rows, LANES), F32),
               pltpu.VMEM((TM, e), F32)]
    kernel = functools.partial(_conformer_kernel, seq=seq)
    return _layer_call(kernel, "conformer_layer", x, mod, ng, weights, scratch)


def _shortconv_kernel(xp_ref, x_ref, xn_ref, mod_ref, ng_ref, w_in_ref, cw_ref, w_out_ref,
                      o_ref, h_scr, q_scr, acc_scr, *, seq):
    nblk = w_out_ref.shape[0]
    e = nblk * CB
    _fill_h(h_scr, [xp_ref, x_ref, xn_ref], ng_ref, mod_ref)
    pad = SHORT_CONV_WIDTH // 2
    for j in range(nblk):
        hx = h_scr[...]
        cg = _dot(hx, w_in_ref[:, e + j * CB:e + (j + 1) * CB])
        v = _dot(hx, w_in_ref[:, 2 * e + j * CB:2 * e + (j + 1) * CB])
        q_buf = q_scr.at[j % 2]
        _store_slabs(q_buf, _zero_outside_sequence(cg * v, seq, pl.program_id(1)))
        cw = cw_ref[j]
        conv = cw[0:1, :] * _shifted(q_buf, HALO - pad)
        for k in range(1, SHORT_CONV_WIDTH):
            conv = conv + cw[k:k + 1, :] * _shifted(q_buf, HALO - pad + k)
        hm = h_scr[pl.ds(HALO, TM), :]
        bg = _dot(hm, w_in_ref[:, j * CB:(j + 1) * CB])
        z = _dot(hm, w_in_ref[:, 3 * e + j * CB:3 * e + (j + 1) * CB])
        y = (bg * conv * _silu(z)).astype(BF16)
        _accumulate(acc_scr, j, _dot(y, w_out_ref[j]))
    o_ref[0] = _residual(x_ref, mod_ref, acc_scr[...])


def _shortconv_layer(x, mod, ng, w_in, conv_w, w_out):
    d = x.shape[-1]
    seq = x.shape[1]
    weights = [
        w_in.astype(BF16),
        _split_vec(conv_w, CB),
        _split_rows(w_out, CB),
    ]
    rows = TM + 2 * HALO
    scratch = [pltpu.VMEM((rows, d), BF16),
               pltpu.VMEM((2, CB // LANES, rows, LANES), F32),
               pltpu.VMEM((TM, d), F32)]
    kernel = functools.partial(_shortconv_kernel, seq=seq)
    return _layer_call(kernel, "shortconv_layer", x, mod, ng, weights, scratch)


def _pool_kernel(xp_ref, x_ref, xn_ref, mod_ref, ng_ref, w_in_ref, pw_ref, pb_ref,
                 sc_ref, w_out_ref, fg_ref, o_ref, h_scr, p_scr, *, seq):
    ngrp, grp, _ = pw_ref.shape
    e = ngrp * grp
    _fill_h(h_scr, [xp_ref, x_ref, xn_ref], ng_ref, mod_ref)
    t = pl.program_id(1) * TM + lax.broadcasted_iota(jnp.int32, (TM, 1), 0)

    sz = _silu(_dot(h_scr[pl.ds(HALO, TM), :], w_in_ref[:, e:2 * e]))
    _store_slabs(p_scr, _zero_outside_sequence(_dot(h_scr[...], w_in_ref[:, 0:e]), seq,
                                               pl.program_id(1)))
    per = grp // LANES
    parts = []
    for gi, w in enumerate(POOL_WINDOWS):
        lo = jnp.maximum(t - w // 2, 0)
        hi = jnp.minimum(t + w - 1 - w // 2, seq - 1)
        inv_cnt = 1.0 / (hi - lo + 1).astype(F32)
        p_buf = p_scr.at[pl.ds(gi * per, per)]
        win = _window_sum(p_buf, HALO - w // 2, w)
        dlt = (win * inv_cnt - _shifted(p_buf, HALO)).astype(BF16)
        pooled = (_dot(dlt, pw_ref[gi]) + pb_ref[gi]) * sc_ref[gi]
        parts.append((pooled * sz[:, gi * grp:(gi + 1) * grp]).astype(BF16))
    y = jnp.concatenate(parts, axis=1)
    out = _residual(x_ref, mod_ref, _dot(y, w_out_ref[...]))
    ms = jnp.mean(out * out, axis=-1, keepdims=True)
    o_ref[0] = (out * lax.rsqrt(ms + EPS)) * fg_ref[...]


def _pool_layer(x, mod, ng, w_in, pool_w, pool_b, scale, w_out, final_g):
    d = x.shape[-1]
    seq = x.shape[1]
    e = w_out.shape[0]
    ngrp = len(POOL_WINDOWS)
    grp = e // ngrp
    weights = [
        w_in.astype(BF16),
        pool_w.astype(BF16), pool_b.reshape(ngrp, 1, grp), scale.reshape(ngrp, 1, grp),
        w_out.astype(BF16),
        final_g.reshape(1, d),
    ]
    rows = TM + 2 * HALO
    scratch = [pltpu.VMEM((rows, d), BF16),
               pltpu.VMEM((e // LANES, rows, LANES), F32)]
    kernel = functools.partial(_pool_kernel, seq=seq)
    return _layer_call(kernel, "pool_layer", x, mod, ng, weights, scratch)


def kernel(x, c, norm_g, ada_w, ada_b, gm_w_in, gm_ln_g, gm_ln_b, gm_sp_w, gm_sp_b, gm_w_out, cv_w_in, cv_dw_w, cv_dw_b, cv_ln_g, cv_ln_b, cv_w_out, sc_w_in, sc_conv_w, sc_w_out, pl_w_in, pl_w, pl_b, pl_scale, pl_w_out, final_g):
    bsz, seq, d = x.shape
    assert seq % TM == 0 and TM % CHUNK == 0 and TM % HALO == 0
    assert norm_g.shape[0] == 4, "one layer of each mixer kind, final norm fused in the last"
    mod = _ada_mod(c, ada_w, ada_b).reshape(norm_g.shape[0], bsz, 3, d)
    x = _gmlp_layer(x, mod[0], norm_g[0], gm_w_in[0], gm_ln_g[0], gm_ln_b[0],
                    gm_sp_w[0], gm_sp_b[0], gm_w_out[0])
    x = _conformer_layer(x, mod[1], norm_g[1], cv_w_in[0], cv_dw_w[0], cv_dw_b[0],
                         cv_ln_g[0], cv_ln_b[0], cv_w_out[0])
    x = _shortconv_layer(x, mod[2], norm_g[2], sc_w_in[0], sc_conv_w[0], sc_w_out[0])
    return _pool_layer(x, mod[3], norm_g[3], pl_w_in[0], pl_w[0], pl_b[0], pl_scale[0],
                       pl_w_out[0], final_g)
```

```python
import functools
import math

import jax
import jax.numpy as jnp
from jax import lax
from jax.experimental import pallas as pl
from jax.experimental.pallas import tpu as pltpu

EPS = 1e-6
CHUNK = 128
GM_HEADS = 8
CONV_WIDTH = 31
SHORT_CONV_WIDTH = 3
POOL_WINDOWS = (2, 4, 8, 16)

TM = 512
HALO = 16
CB = 1024
CONV_ROWS = 64
LANES = 128
VMEM_LIMIT_BYTES = 56 * 1024 * 1024

BF16 = jnp.bfloat16
F32 = jnp.float32


def _dot(a, b):
    return jnp.dot(a, b, preferred_element_type=F32)


def _sigmoid(x):
    return 0.5 + 0.5 * jnp.tanh(0.5 * x)


def _silu(x):
    h = 0.5 * x
    return h + h * jnp.tanh(h)


def _gelu_tanh(x):
    c = math.sqrt(2.0 / math.pi)
    return 0.5 * x * (1.0 + jnp.tanh(c * (x + 0.044715 * (x * x * x))))


def _modulated_norm(x, gain, shift):
    ms = jnp.mean(x * x, axis=-1, keepdims=True)
    return (x * lax.rsqrt(ms + EPS)) * gain + shift


def _zero_outside_sequence(val, seq, tile_in_seq):
    def masked(r0):
        t = tile_in_seq * TM - HALO + r0 + lax.broadcasted_iota(jnp.int32, (HALO, 1), 0)
        inside = jnp.logical_and(t >= 0, t < seq)
        return jnp.where(inside, val[r0:r0 + HALO], 0.0)

    return jnp.concatenate([masked(0), val[HALO:HALO + TM], masked(HALO + TM)], axis=0)


def _ada_kernel(c_ref, w_ref, b_ref, o_ref):
    c_act = _silu(c_ref[...]).astype(BF16)
    o_ref[0] = _dot(c_act, w_ref[0].astype(BF16)) + b_ref[0]


def _ada_mod(c, ada_w, ada_b):
    depth, d, n = ada_w.shape
    bsz = c.shape[0]
    nb = d
    return pl.pallas_call(
        _ada_kernel,
        out_shape=jax.ShapeDtypeStruct((depth, bsz, n), F32),
        grid=(depth, n // nb),
        in_specs=[
            pl.BlockSpec((bsz, d), lambda i, j: (0, 0)),
            pl.BlockSpec((1, d, nb), lambda i, j: (i, 0, j)),
            pl.BlockSpec((1, 1, nb), lambda i, j: (i, 0, j)),
        ],
        out_specs=pl.BlockSpec((1, bsz, nb), lambda i, j: (i, 0, j)),
        compiler_params=pltpu.CompilerParams(
            dimension_semantics=("arbitrary", "arbitrary")),
        name="ada_mod",
    )(c, ada_w, ada_b.reshape(depth, 1, n))


def _norm_params(ng_ref, mod_ref):
    shift = mod_ref[0, 0:1, :]
    gain = ng_ref[...] * (1.0 + mod_ref[0, 1:2, :])
    return gain, shift


def _fill_h(h_scr, x_refs, ng_ref, mod_ref):
    gain, shift = _norm_params(ng_ref, mod_ref)
    r0 = 0
    for xr in x_refs:
        rows = xr.shape[1]
        h_scr[pl.ds(r0, rows), :] = _modulated_norm(xr[0], gain, shift).astype(BF16)
        r0 += rows


def _residual(x_ref, mod_ref, acc):
    return x_ref[0] + mod_ref[0, 2:3, :] * acc


def _store_slabs(scr, val):
    for s in range(scr.shape[0]):
        scr[s] = val[:, s * LANES:(s + 1) * LANES]


def _shifted(scr, first_row, nrows=TM):
    return jnp.concatenate(
        [scr[s, pl.ds(first_row, nrows), :] for s in range(scr.shape[0])], axis=1)


def _window_sum(scr, first_row, w):
    half = 8
    if w == 2 * half:
        part = _shifted(scr, first_row, TM + half)
        for k in range(1, half):
            part = part + _shifted(scr, first_row + k, TM + half)
        return part[0:TM] + part[half:TM + half]
    win = _shifted(scr, first_row)
    for k in range(1, w):
        win = win + _shifted(scr, first_row + k)
    return win


def _const_spec(shape):
    nd = len(shape)
    return pl.BlockSpec(shape, lambda *_: (0,) * nd, pipeline_mode=pl.Buffered(1))


def _halo_specs(d, seq):
    per_tile = TM // HALO
    last = seq // HALO - 1
    before = pl.BlockSpec((1, HALO, d),
                          lambda b, s: (b, jnp.maximum(s * per_tile - 1, 0), 0))
    after = pl.BlockSpec((1, HALO, d),
                         lambda b, s: (b, jnp.minimum((s + 1) * per_tile, last), 0))
    return before, after


def _with_casts(kernel, n_in, n_cast):
    def wrapped(*refs):
        ins, rest = refs[:n_in], refs[n_in:]
        srcs, out, dsts, scratch = (rest[:n_cast], rest[n_cast], rest[n_cast + 1:2 * n_cast + 1],
                                    rest[2 * n_cast + 1:])
        kernel(*ins, out, *scratch)
        for src, dst in zip(srcs, dsts):
            dst[...] = src[...].astype(BF16)
    return wrapped


def _layer_call(kernel, name, x, mod, ng, weights, scratch, halo=True, cast=()):
    bsz, seq, d = x.shape
    tps = seq // TM
    x_specs = [pl.BlockSpec((1, TM, d), lambda b, s: (b, s, 0))]
    if halo:
        before, after = _halo_specs(d, seq)
        x_specs = [before] + x_specs + [after]
    in_specs = (x_specs
                + [pl.BlockSpec((1, 3, d), lambda b, s: (b, 0, 0)),
                   _const_spec((1, d))]
                + [_const_spec(w.shape) for w in weights])
    cast_specs = [pl.BlockSpec((w.shape[0] // (bsz * tps), w.shape[1]),
                               lambda b, s: (b * tps + s, 0)) for w in cast]
    if cast:
        kernel = _with_casts(kernel, len(in_specs), len(cast))
    return pl.pallas_call(
        kernel,
        out_shape=[jax.ShapeDtypeStruct(x.shape, x.dtype)]
        + [jax.ShapeDtypeStruct(w.shape, BF16) for w in cast],
        grid=(bsz, tps),
        in_specs=in_specs + cast_specs,
        out_specs=[pl.BlockSpec((1, TM, d), lambda b, s: (b, s, 0))] + cast_specs,
        scratch_shapes=scratch,
        compiler_params=pltpu.CompilerParams(
            dimension_semantics=("arbitrary", "arbitrary"),
            vmem_limit_bytes=VMEM_LIMIT_BYTES),
        name=name,
    )(*([x] * len(x_specs)), mod, ng.reshape(1, d), *weights, *cast)


def _split_rows(w, blk):
    e, d = w.shape
    return w.reshape(e // blk, blk, d)


def _split_vec(v, blk):
    k, e = v.shape
    return v.reshape(k, e // blk, blk).transpose(1, 0, 2)


def _layernorm_stats(s1, s2, e):
    mu = s1 / e
    return mu, lax.rsqrt(s2 / e - mu * mu + EPS)


def _accumulate(acc_scr, j, part):
    if j == 0:
        acc_scr[...] = part
    else:
        acc_scr[...] += part


def _gmlp_kernel(x_ref, mod_ref, ng_ref, w_in_ref, lng_ref, lnb_ref,
                 spw_ref, spb_ref, w_out_ref, o_ref, h_scr, v_scr, g_scr):
    e = w_out_ref.shape[0]
    hd = e // GM_HEADS
    _fill_h(h_scr, [x_ref], ng_ref, mod_ref)

    hx = h_scr[...]
    v = _gelu_tanh(_dot(hx, w_in_ref[:, e:2 * e]))
    v_scr[...] = v
    mu, rstd = _layernorm_stats(jnp.sum(v, axis=-1, keepdims=True),
                                jnp.sum(v * v, axis=-1, keepdims=True), e)
    u = _gelu_tanh(_dot(hx, w_in_ref[:, 0:e]))
    z = _dot(hx, w_in_ref[:, 2 * e:3 * e])
    g_scr[...] = u * _silu(z)

    vn = ((v_scr[...] - mu) * rstd * lng_ref[...] + lnb_ref[...]).astype(BF16)
    n_chunks = TM // CHUNK
    heads = []
    for g in range(GM_HEADS):
        vh = vn[:, g * hd:(g + 1) * hd]
        wide = jnp.concatenate(
            [vh[n * CHUNK:(n + 1) * CHUNK, :] for n in range(n_chunks)], axis=1)
        mixed = _dot(spw_ref[g], wide) + spb_ref[g]
        heads.append(jnp.concatenate(
            [mixed[:, n * hd:(n + 1) * hd] for n in range(n_chunks)], axis=0))
    vs = jnp.concatenate(heads, axis=1)
    y = (g_scr[...] * vs).astype(BF16)
    o_ref[0] = _residual(x_ref, mod_ref, _dot(y, w_out_ref[...]))


def _gmlp_layer(x, mod, ng, w_in, ln_g, ln_b, sp_w, sp_b, w_out, cast):
    d = x.shape[-1]
    e = w_out.shape[0]
    weights = [
        w_in,
        ln_g.reshape(1, e), ln_b.reshape(1, e),
        sp_w.astype(BF16), sp_b.reshape(GM_HEADS, CHUNK, 1),
        w_out,
    ]
    scratch = [pltpu.VMEM((TM, d), BF16),
               pltpu.VMEM((TM, e), F32),
               pltpu.VMEM((TM, e), F32)]
    return _layer_call(_gmlp_kernel, "gmlp_layer", x, mod, ng, weights, scratch, halo=False,
                       cast=cast)


def _depthwise_conv(y_scr, dw_ref, dwb_ref, first_row, c_scr):
    for s in range(y_scr.shape[0]):
        lanes = pl.ds(s * LANES, LANES)
        bias = dwb_ref[:, lanes]
        for r0 in range(0, TM, CONV_ROWS):
            acc = jnp.broadcast_to(bias, (CONV_ROWS, LANES))
            for k in range(CONV_WIDTH):
                acc = acc + (dw_ref[pl.ds(k, 1), lanes]
                             * y_scr[s, pl.ds(first_row + r0 + k, CONV_ROWS), :])
            c_scr[pl.ds(r0, CONV_ROWS), lanes] = acc


def _conformer_kernel(xp_ref, x_ref, xn_ref, mod_ref, ng_ref, w_in_ref,
                      dw_ref, dwb_ref, lng_ref, lnb_ref, w_out_ref, o_ref,
                      h_scr, y_scr, c_scr, *, seq):
    e = w_out_ref.shape[0]
    _fill_h(h_scr, [xp_ref, x_ref, xn_ref], ng_ref, mod_ref)
    pad = CONV_WIDTH // 2

    hx = h_scr[...]
    a = _dot(hx, w_in_ref[:, 0:e])
    g = _dot(hx, w_in_ref[:, e:2 * e])
    _store_slabs(y_scr, _zero_outside_sequence(a * _sigmoid(g), seq, pl.program_id(1)))
    _depthwise_conv(y_scr, dw_ref, dwb_ref, HALO - pad, c_scr)
    c = c_scr[...]
    mu, rstd = _layernorm_stats(jnp.sum(c, axis=-1, keepdims=True),
                                jnp.sum(c * c, axis=-1, keepdims=True), e)
    cn = (c - mu) * rstd * lng_ref[...] + lnb_ref[...]
    z = _dot(h_scr[pl.ds(HALO, TM), :], w_in_ref[:, 2 * e:3 * e])
    y = (_silu(cn) * _silu(z)).astype(BF16)
    o_ref[0] = _residual(x_ref, mod_ref, _dot(y, w_out_ref[...]))


def _conformer_layer(x, mod, ng, w_in, dw_w, dw_b, ln_g, ln_b, w_out, cast):
    d = x.shape[-1]
    seq = x.shape[1]
    e = w_out.shape[0]
    weights = [
        w_in,
        dw_w, dw_b.reshape(1, e), ln_g.reshape(1, e), ln_b.reshape(1, e),
        w_out,
    ]
    rows = TM + 2 * HALO
    scratch = [pltpu.VMEM((rows, d), BF16),
               pltpu.VMEM((e // LANES, rows, LANES), F32),
               pltpu.VMEM((TM, e), F32)]
    kernel = functools.partial(_conformer_kernel, seq=seq)
    return _layer_call(kernel, "conformer_layer", x, mod, ng, weights, scratch, cast=cast)


def _shortconv_kernel(xp_ref, x_ref, xn_ref, mod_ref, ng_ref, w_in_ref, cw_ref, w_out_ref,
                      o_ref, h_scr, q_scr, acc_scr, *, seq):
    nblk = w_out_ref.shape[0]
    e = nblk * CB
    _fill_h(h_scr, [xp_ref, x_ref, xn_ref], ng_ref, mod_ref)
    pad = SHORT_CONV_WIDTH // 2
    for j in range(nblk):
        hx = h_scr[...]
        cg = _dot(hx, w_in_ref[:, e + j * CB:e + (j + 1) * CB])
        v = _dot(hx, w_in_ref[:, 2 * e + j * CB:2 * e + (j + 1) * CB])
        q_buf = q_scr.at[j % 2]
        _store_slabs(q_buf, _zero_outside_sequence(cg * v, seq, pl.program_id(1)))
        cw = cw_ref[j]
        conv = cw[0:1, :] * _shifted(q_buf, HALO - pad)
        for k in range(1, SHORT_CONV_WIDTH):
            conv = conv + cw[k:k + 1, :] * _shifted(q_buf, HALO - pad + k)
        hm = h_scr[pl.ds(HALO, TM), :]
        bg = _dot(hm, w_in_ref[:, j * CB:(j + 1) * CB])
        z = _dot(hm, w_in_ref[:, 3 * e + j * CB:3 * e + (j + 1) * CB])
        y = (bg * conv * _silu(z)).astype(BF16)
        _accumulate(acc_scr, j, _dot(y, w_out_ref[j]))
    o_ref[0] = _residual(x_ref, mod_ref, acc_scr[...])


def _shortconv_layer(x, mod, ng, w_in, conv_w, w_out, cast):
    d = x.shape[-1]
    seq = x.shape[1]
    weights = [
        w_in,
        _split_vec(conv_w, CB),
        _split_rows(w_out, CB),
    ]
    rows = TM + 2 * HALO
    scratch = [pltpu.VMEM((rows, d), BF16),
               pltpu.VMEM((2, CB // LANES, rows, LANES), F32),
               pltpu.VMEM((TM, d), F32)]
    kernel = functools.partial(_shortconv_kernel, seq=seq)
    return _layer_call(kernel, "shortconv_layer", x, mod, ng, weights, scratch, cast=cast)


def _pool_kernel(xp_ref, x_ref, xn_ref, mod_ref, ng_ref, w_in_ref, pw_ref, pb_ref,
                 sc_ref, w_out_ref, fg_ref, o_ref, h_scr, p_scr, *, seq):
    ngrp, grp, _ = pw_ref.shape
    e = ngrp * grp
    _fill_h(h_scr, [xp_ref, x_ref, xn_ref], ng_ref, mod_ref)
    t = pl.program_id(1) * TM + lax.broadcasted_iota(jnp.int32, (TM, 1), 0)

    sz = _silu(_dot(h_scr[pl.ds(HALO, TM), :], w_in_ref[:, e:2 * e]))
    _store_slabs(p_scr, _zero_outside_sequence(_dot(h_scr[...], w_in_ref[:, 0:e]), seq,
                                               pl.program_id(1)))
    per = grp // LANES
    parts = []
    for gi, w in enumerate(POOL_WINDOWS):
        lo = jnp.maximum(t - w // 2, 0)
        hi = jnp.minimum(t + w - 1 - w // 2, seq - 1)
        inv_cnt = 1.0 / (hi - lo + 1).astype(F32)
        p_buf = p_scr.at[pl.ds(gi * per, per)]
        win = _window_sum(p_buf, HALO - w // 2, w)
        dlt = (win * inv_cnt - _shifted(p_buf, HALO)).astype(BF16)
        pooled = (_dot(dlt, pw_ref[gi]) + pb_ref[gi]) * sc_ref[gi]
        parts.append((pooled * sz[:, gi * grp:(gi + 1) * grp]).astype(BF16))
    y = jnp.concatenate(parts, axis=1)
    out = _residual(x_ref, mod_ref, _dot(y, w_out_ref[...]))
    ms = jnp.mean(out * out, axis=-1, keepdims=True)
    o_ref[0] = (out * lax.rsqrt(ms + EPS)) * fg_ref[...]


def _pool_layer(x, mod, ng, w_in, pool_w, pool_b, scale, w_out, final_g):
    d = x.shape[-1]
    seq = x.shape[1]
    e = w_out.shape[0]
    ngrp = len(POOL_WINDOWS)
    grp = e // ngrp
    weights = [
        w_in,
        pool_w.astype(BF16), pool_b.reshape(ngrp, 1, grp), scale.reshape(ngrp, 1, grp),
        w_out,
        final_g.reshape(1, d),
    ]
    rows = TM + 2 * HALO
    scratch = [pltpu.VMEM((rows, d), BF16),
               pltpu.VMEM((e // LANES, rows, LANES), F32)]
    kernel = functools.partial(_pool_kernel, seq=seq)
    return _layer_call(kernel, "pool_layer", x, mod, ng, weights, scratch)


def kernel(x, c, norm_g, ada_w, ada_b, gm_w_in, gm_ln_g, gm_ln_b, gm_sp_w, gm_sp_b, gm_w_out, cv_w_in, cv_dw_w, cv_dw_b, cv_ln_g, cv_ln_b, cv_w_out, sc_w_in, sc_conv_w, sc_w_out, pl_w_in, pl_w, pl_b, pl_scale, pl_w_out, final_g):
    bsz, seq, d = x.shape
    assert seq % TM == 0 and TM % CHUNK == 0 and TM % HALO == 0
    assert norm_g.shape[0] == 4, "one layer of each mixer kind, final norm fused in the last"
    mod = _ada_mod(c, ada_w, ada_b).reshape(norm_g.shape[0], bsz, 3, d)
    x, cv_w_in_b, cv_w_out_b = _gmlp_layer(
        x, mod[0], norm_g[0], gm_w_in[0].astype(BF16), gm_ln_g[0], gm_ln_b[0], gm_sp_w[0],
        gm_sp_b[0], gm_w_out[0].astype(BF16), cast=(cv_w_in[0], cv_w_out[0]))
    x, sc_w_in_b, sc_w_out_b = _conformer_layer(
        x, mod[1], norm_g[1], cv_w_in_b, cv_dw_w[0], cv_dw_b[0], cv_ln_g[0], cv_ln_b[0],
        cv_w_out_b, cast=(sc_w_in[0], sc_w_out[0]))
    x, pl_w_in_b, pl_w_out_b = _shortconv_layer(
        x, mod[2], norm_g[2], sc_w_in_b, sc_conv_w[0], sc_w_out_b,
        cast=(pl_w_in[0], pl_w_out[0]))
    (out,) = _pool_layer(x, mod[3], norm_g[3], pl_w_in_b, pl_w[0], pl_b[0], pl_scale[0],
                         pl_w_out_b, final_g)
    return out
```

```python
import functools
import math

import jax
import jax.numpy as jnp
from jax import lax
from jax.experimental import pallas as pl
from jax.experimental.pallas import tpu as pltpu

EPS = 1e-6
CHUNK = 128
GM_HEADS = 8
CONV_WIDTH = 31
SHORT_CONV_WIDTH = 3
POOL_WINDOWS = (2, 4, 8, 16)

TM = 512
HALO = 16
CB = 1024
CONV_ROWS = 64
LANES = 128
VMEM_LIMIT_BYTES = 56 * 1024 * 1024

BF16 = jnp.bfloat16
F32 = jnp.float32


def _dot(a, b):
    return jnp.dot(a, b, preferred_element_type=F32)


def _sigmoid(x):
    return 0.5 + 0.5 * jnp.tanh(0.5 * x)


def _silu(x):
    h = 0.5 * x
    return h + h * jnp.tanh(h)


def _gelu_tanh(x):
    c = math.sqrt(2.0 / math.pi)
    return 0.5 * x * (1.0 + jnp.tanh(c * (x + 0.044715 * (x * x * x))))


def _modulated_norm(x, gain, shift):
    ms = jnp.mean(x * x, axis=-1, keepdims=True)
    return (x * lax.rsqrt(ms + EPS)) * gain + shift


def _zero_outside_sequence(val, seq, tile_in_seq):
    def masked(r0):
        t = tile_in_seq * TM - HALO + r0 + lax.broadcasted_iota(jnp.int32, (HALO, 1), 0)
        inside = jnp.logical_and(t >= 0, t < seq)
        return jnp.where(inside, val[r0:r0 + HALO], 0.0)

    return jnp.concatenate([masked(0), val[HALO:HALO + TM], masked(HALO + TM)], axis=0)


def _ada_kernel(c_ref, w_ref, b_ref, o_ref):
    c_act = _silu(c_ref[...]).astype(BF16)
    o_ref[0] = _dot(c_act, w_ref[0].astype(BF16)) + b_ref[0]


def _ada_mod(c, ada_w, ada_b):
    depth, d, n = ada_w.shape
    bsz = c.shape[0]
    nb = d
    return pl.pallas_call(
        _ada_kernel,
        out_shape=jax.ShapeDtypeStruct((depth, bsz, n), F32),
        grid=(depth, n // nb),
        in_specs=[
            pl.BlockSpec((bsz, d), lambda i, j: (0, 0)),
            pl.BlockSpec((1, d, nb), lambda i, j: (i, 0, j)),
            pl.BlockSpec((1, 1, nb), lambda i, j: (i, 0, j)),
        ],
        out_specs=pl.BlockSpec((1, bsz, nb), lambda i, j: (i, 0, j)),
        compiler_params=pltpu.CompilerParams(
            dimension_semantics=("arbitrary", "arbitrary")),
        name="ada_mod",
    )(c, ada_w, ada_b.reshape(depth, 1, n))


def _norm_params(ng_ref, mod_ref):
    shift = mod_ref[0, 0:1, :]
    gain = ng_ref[...] * (1.0 + mod_ref[0, 1:2, :])
    return gain, shift


def _fill_h(h_scr, x_refs, ng_ref, mod_ref):
    gain, shift = _norm_params(ng_ref, mod_ref)
    r0 = 0
    for xr in x_refs:
        rows = xr.shape[1]
        h_scr[pl.ds(r0, rows), :] = _modulated_norm(xr[0], gain, shift).astype(BF16)
        r0 += rows


def _residual(x_ref, mod_ref, acc):
    return x_ref[0] + mod_ref[0, 2:3, :] * acc


def _zero_from(tile):
    bits = lax.bitcast_convert_type(tile, jnp.int32)
    return lax.shift_right_logical(lax.shift_right_logical(bits, 16), 16).astype(F32)


def _store_slabs(scr, val):
    for s in range(scr.shape[0]):
        scr[s] = val[:, s * LANES:(s + 1) * LANES]


def _shifted(scr, first_row, nrows=TM):
    return jnp.concatenate(
        [scr[s, pl.ds(first_row, nrows), :] for s in range(scr.shape[0])], axis=1)


def _window_sum(scr, first_row, w):
    half = 8
    if w == 2 * half:
        part = _shifted(scr, first_row, TM + half)
        for k in range(1, half):
            part = part + _shifted(scr, first_row + k, TM + half)
        return part[0:TM] + part[half:TM + half]
    win = _shifted(scr, first_row)
    for k in range(1, w):
        win = win + _shifted(scr, first_row + k)
    return win


def _const_spec(shape):
    nd = len(shape)
    return pl.BlockSpec(shape, lambda *_: (0,) * nd, pipeline_mode=pl.Buffered(1))


def _halo_specs(d, seq):
    per_tile = TM // HALO
    last = seq // HALO - 1
    before = pl.BlockSpec((1, HALO, d),
                          lambda b, s: (b, jnp.maximum(s * per_tile - 1, 0), 0))
    after = pl.BlockSpec((1, HALO, d),
                         lambda b, s: (b, jnp.minimum((s + 1) * per_tile, last), 0))
    return before, after


def _with_casts(kernel, n_in, n_cast):
    def wrapped(*refs):
        ins, rest = refs[:n_in], refs[n_in:]
        srcs, out, dsts, scratch = (rest[:n_cast], rest[n_cast], rest[n_cast + 1:2 * n_cast + 1],
                                    rest[2 * n_cast + 1:])
        kernel(*ins, out, *scratch)
        for src, dst in zip(srcs, dsts):
            dst[...] = src[...].astype(BF16)
    return wrapped


def _layer_call(kernel, name, x, mod, ng, weights, scratch, halo=True, cast=()):
    bsz, seq, d = x.shape
    tps = seq // TM
    x_specs = [pl.BlockSpec((1, TM, d), lambda b, s: (b, s, 0))]
    if halo:
        before, after = _halo_specs(d, seq)
        x_specs = [before] + x_specs + [after]
    in_specs = (x_specs
                + [pl.BlockSpec((1, 3, d), lambda b, s: (b, 0, 0)),
                   _const_spec((1, d))]
                + [_const_spec(w.shape) for w in weights])
    cast_specs = [pl.BlockSpec((w.shape[0] // (bsz * tps), w.shape[1]),
                               lambda b, s: (b * tps + s, 0)) for w in cast]
    if cast:
        kernel = _with_casts(kernel, len(in_specs), len(cast))
    return pl.pallas_call(
        kernel,
        out_shape=[jax.ShapeDtypeStruct(x.shape, x.dtype)]
        + [jax.ShapeDtypeStruct(w.shape, BF16) for w in cast],
        grid=(bsz, tps),
        in_specs=in_specs + cast_specs,
        out_specs=[pl.BlockSpec((1, TM, d), lambda b, s: (b, s, 0))] + cast_specs,
        scratch_shapes=scratch,
        compiler_params=pltpu.CompilerParams(
            dimension_semantics=("arbitrary", "arbitrary"),
            vmem_limit_bytes=VMEM_LIMIT_BYTES),
        name=name,
    )(*([x] * len(x_specs)), mod, ng.reshape(1, d), *weights, *cast)


def _split_rows(w, blk):
    e, d = w.shape
    return w.reshape(e // blk, blk, d)


def _split_vec(v, blk):
    k, e = v.shape
    return v.reshape(k, e // blk, blk).transpose(1, 0, 2)


def _layernorm_stats(s1, s2, e):
    mu = s1 / e
    return mu, lax.rsqrt(s2 / e - mu * mu + EPS)


def _accumulate(acc_scr, j, part):
    if j == 0:
        acc_scr[...] = part
    else:
        acc_scr[...] += part


def _gmlp_kernel(x_ref, mod_ref, ng_ref, w_in_ref, lng_ref, lnb_ref,
                 spw_ref, spb_ref, w_out_ref, o_ref, h_scr, v_scr, g_scr):
    e = w_out_ref.shape[0]
    hd = e // GM_HEADS
    _fill_h(h_scr, [x_ref], ng_ref, mod_ref)

    hx = h_scr[...]
    v = _gelu_tanh(_dot(hx, w_in_ref[:, e:2 * e]))
    v_scr[...] = v
    mu, rstd = _layernorm_stats(jnp.sum(v, axis=-1, keepdims=True),
                                jnp.sum(v * v, axis=-1, keepdims=True), e)
    u = _gelu_tanh(_dot(hx, w_in_ref[:, 0:e]))
    z = _dot(hx, w_in_ref[:, 2 * e:3 * e])
    g_scr[...] = u * _silu(z)

    vn = ((v_scr[...] - mu) * rstd * lng_ref[...] + lnb_ref[...]).astype(BF16)
    n_chunks = TM // CHUNK
    heads = []
    for g in range(GM_HEADS):
        vh = vn[:, g * hd:(g + 1) * hd]
        wide = jnp.concatenate(
            [vh[n * CHUNK:(n + 1) * CHUNK, :] for n in range(n_chunks)], axis=1)
        mixed = _dot(spw_ref[g], wide) + spb_ref[g]
        heads.append(jnp.concatenate(
            [mixed[:, n * hd:(n + 1) * hd] for n in range(n_chunks)], axis=0))
    vs = jnp.concatenate(heads, axis=1)
    y = (g_scr[...] * vs).astype(BF16)
    o_ref[0] = _residual(x_ref, mod_ref, _dot(y, w_out_ref[...]))


def _gmlp_layer(x, mod, ng, w_in, ln_g, ln_b, sp_w, sp_b, w_out, cast):
    d = x.shape[-1]
    e = w_out.shape[0]
    weights = [
        w_in,
        ln_g.reshape(1, e), ln_b.reshape(1, e),
        sp_w.astype(BF16), sp_b.reshape(GM_HEADS, CHUNK, 1),
        w_out,
    ]
    scratch = [pltpu.VMEM((TM, d), BF16),
               pltpu.VMEM((TM, e), F32),
               pltpu.VMEM((TM, e), F32)]
    return _layer_call(_gmlp_kernel, "gmlp_layer", x, mod, ng, weights, scratch, halo=False,
                       cast=cast)


def _depthwise_conv(y_scr, dw_ref, dwb_ref, first_row, c_scr):
    for s in range(y_scr.shape[0]):
        lanes = pl.ds(s * LANES, LANES)
        bias = dwb_ref[:, lanes]
        for r0 in range(0, TM, CONV_ROWS):
            acc = jnp.broadcast_to(bias, (CONV_ROWS, LANES))
            for k in range(CONV_WIDTH):
                acc = acc + (dw_ref[pl.ds(k, 1), lanes]
                             * y_scr[s, pl.ds(first_row + r0 + k, CONV_ROWS), :])
            c_scr[pl.ds(r0, CONV_ROWS), lanes] = acc


def _conformer_kernel(xp_ref, x_ref, xn_ref, mod_ref, ng_ref, w_in_ref,
                      dw_ref, dwb_ref, lng_ref, lnb_ref, w_out_ref, o_ref,
                      h_scr, y_scr, c_scr, *, seq):
    e = w_out_ref.shape[0]
    _fill_h(h_scr, [xp_ref, x_ref, xn_ref], ng_ref, mod_ref)
    pad = CONV_WIDTH // 2

    hx = h_scr[...]
    a = _dot(hx, w_in_ref[:, 0:e])
    g = _dot(hx, w_in_ref[:, e:2 * e])
    _store_slabs(y_scr, _zero_outside_sequence(a * _sigmoid(g), seq, pl.program_id(1)))
    _depthwise_conv(y_scr, dw_ref, dwb_ref, HALO - pad, c_scr)
    mid = (y_scr.shape[0] // 2) * LANES
    zero = _zero_from(c_scr[0:8, mid:mid + LANES])
    anchor = h_scr[pl.ds(HALO, HALO), 0:LANES].astype(F32) + jnp.concatenate([zero, zero], axis=0)
    h_scr[pl.ds(HALO, HALO), 0:LANES] = anchor.astype(BF16)
    c = c_scr[...]
    mu, rstd = _layernorm_stats(jnp.sum(c, axis=-1, keepdims=True),
                                jnp.sum(c * c, axis=-1, keepdims=True), e)
    cn = (c - mu) * rstd * lng_ref[...] + lnb_ref[...]
    z = _dot(h_scr[pl.ds(HALO, TM), :], w_in_ref[:, 2 * e:3 * e])
    y = (_silu(cn) * _silu(z)).astype(BF16)
    o_ref[0] = _residual(x_ref, mod_ref, _dot(y, w_out_ref[...]))


def _conformer_layer(x, mod, ng, w_in, dw_w, dw_b, ln_g, ln_b, w_out, cast):
    d = x.shape[-1]
    seq = x.shape[1]
    e = w_out.shape[0]
    weights = [
        w_in,
        dw_w, dw_b.reshape(1, e), ln_g.reshape(1, e), ln_b.reshape(1, e),
        w_out,
    ]
    rows = TM + 2 * HALO
    scratch = [pltpu.VMEM((rows, d), BF16),
               pltpu.VMEM((e // LANES, rows, LANES), F32),
               pltpu.VMEM((TM, e), F32)]
    kernel = functools.partial(_conformer_kernel, seq=seq)
    return _layer_call(kernel, "conformer_layer", x, mod, ng, weights, scratch, cast=cast)


def _shortconv_kernel(xp_ref, x_ref, xn_ref, mod_ref, ng_ref, w_in_ref, cw_ref, w_out_ref,
                      o_ref, h_scr, q_scr, acc_scr, *, seq):
    nblk = w_out_ref.shape[0]
    e = nblk * CB
    _fill_h(h_scr, [xp_ref, x_ref, xn_ref], ng_ref, mod_ref)
    pad = SHORT_CONV_WIDTH // 2
    for j in range(nblk):
        hx = h_scr[...]
        cg = _dot(hx, w_in_ref[:, e + j * CB:e + (j + 1) * CB])
        v = _dot(hx, w_in_ref[:, 2 * e + j * CB:2 * e + (j + 1) * CB])
        q_buf = q_scr.at[j % 2]
        _store_slabs(q_buf, _zero_outside_sequence(cg * v, seq, pl.program_id(1)))
        cw = cw_ref[j]
        conv = cw[0:1, :] * _shifted(q_buf, HALO - pad)
        for k in range(1, SHORT_CONV_WIDTH):
            conv = conv + cw[k:k + 1, :] * _shifted(q_buf, HALO - pad + k)
        hm = h_scr[pl.ds(HALO, TM), :]
        bg = _dot(hm, w_in_ref[:, j * CB:(j + 1) * CB])
        z = _dot(hm, w_in_ref[:, 3 * e + j * CB:3 * e + (j + 1) * CB])
        y = (bg * conv * _silu(z)).astype(BF16)
        _accumulate(acc_scr, j, _dot(y, w_out_ref[j]))
    o_ref[0] = _residual(x_ref, mod_ref, acc_scr[...])


def _shortconv_layer(x, mod, ng, w_in, conv_w, w_out, cast):
    d = x.shape[-1]
    seq = x.shape[1]
    weights = [
        w_in,
        _split_vec(conv_w, CB),
        _split_rows(w_out, CB),
    ]
    rows = TM + 2 * HALO
    scratch = [pltpu.VMEM((rows, d), BF16),
               pltpu.VMEM((2, CB // LANES, rows, LANES), F32),
               pltpu.VMEM((TM, d), F32)]
    kernel = functools.partial(_shortconv_kernel, seq=seq)
    return _layer_call(kernel, "shortconv_layer", x, mod, ng, weights, scratch, cast=cast)


def _pool_kernel(xp_ref, x_ref, xn_ref, mod_ref, ng_ref, w_in_ref, pw_ref, pb_ref,
                 sc_ref, w_out_ref, fg_ref, o_ref, h_scr, p_scr, *, seq):
    ngrp, grp, _ = pw_ref.shape
    e = ngrp * grp
    _fill_h(h_scr, [xp_ref, x_ref, xn_ref], ng_ref, mod_ref)
    t = pl.program_id(1) * TM + lax.broadcasted_iota(jnp.int32, (TM, 1), 0)

    sz = _silu(_dot(h_scr[pl.ds(HALO, TM), :], w_in_ref[:, e:2 * e]))
    _store_slabs(p_scr, _zero_outside_sequence(_dot(h_scr[...], w_in_ref[:, 0:e]), seq,
                                               pl.program_id(1)))
    per = grp // LANES
    parts = []
    for gi, w in enumerate(POOL_WINDOWS):
        lo = jnp.maximum(t - w // 2, 0)
        hi = jnp.minimum(t + w - 1 - w // 2, seq - 1)
        inv_cnt = 1.0 / (hi - lo + 1).astype(F32)
        p_buf = p_scr.at[pl.ds(gi * per, per)]
        win = _window_sum(p_buf, HALO - w // 2, w)
        dlt = (win * inv_cnt - _shifted(p_buf, HALO)).astype(BF16)
        pooled = (_dot(dlt, pw_ref[gi]) + pb_ref[gi]) * sc_ref[gi]
        parts.append((pooled * sz[:, gi * grp:(gi + 1) * grp]).astype(BF16))
    y = jnp.concatenate(parts, axis=1)
    out = _residual(x_ref, mod_ref, _dot(y, w_out_ref[...]))
    ms = jnp.mean(out * out, axis=-1, keepdims=True)
    o_ref[0] = (out * lax.rsqrt(ms + EPS)) * fg_ref[...]


def _pool_layer(x, mod, ng, w_in, pool_w, pool_b, scale, w_out, final_g):
    d = x.shape[-1]
    seq = x.shape[1]
    e = w_out.shape[0]
    ngrp = len(POOL_WINDOWS)
    grp = e // ngrp
    weights = [
        w_in,
        pool_w.astype(BF16), pool_b.reshape(ngrp, 1, grp), scale.reshape(ngrp, 1, grp),
        w_out,
        final_g.reshape(1, d),
    ]
    rows = TM + 2 * HALO
    scratch = [pltpu.VMEM((rows, d), BF16),
               pltpu.VMEM((e // LANES, rows, LANES), F32)]
    kernel = functools.partial(_pool_kernel, seq=seq)
    return _layer_call(kernel, "pool_layer", x, mod, ng, weights, scratch)


def kernel(x, c, norm_g, ada_w, ada_b, gm_w_in, gm_ln_g, gm_ln_b, gm_sp_w, gm_sp_b, gm_w_out, cv_w_in, cv_dw_w, cv_dw_b, cv_ln_g, cv_ln_b, cv_w_out, sc_w_in, sc_conv_w, sc_w_out, pl_w_in, pl_w, pl_b, pl_scale, pl_w_out, final_g):
    bsz, seq, d = x.shape
    assert seq % TM == 0 and TM % CHUNK == 0 and TM % HALO == 0
    assert norm_g.shape[0] == 4, "one layer of each mixer kind, final norm fused in the last"
    mod = _ada_mod(c, ada_w, ada_b).reshape(norm_g.shape[0], bsz, 3, d)
    x, cv_w_in_b, cv_w_out_b = _gmlp_layer(
        x, mod[0], norm_g[0], gm_w_in[0].astype(BF16), gm_ln_g[0], gm_ln_b[0], gm_sp_w[0],
        gm_sp_b[0], gm_w_out[0].astype(BF16), cast=(cv_w_in[0], cv_w_out[0]))
    x, sc_w_in_b, sc_w_out_b = _conformer_layer(
        x, mod[1], norm_g[1], cv_w_in_b, cv_dw_w[0], cv_dw_b[0], cv_ln_g[0], cv_ln_b[0],
        cv_w_out_b, cast=(sc_w_in[0], sc_w_out[0]))
    x, pl_w_in_b, pl_w_out_b = _shortconv_layer(
        x, mod[2], norm_g[2], sc_w_in_b, sc_conv_w[0], sc_w_out_b,
        cast=(pl_w_in[0], pl_w_out[0]))
    (out,) = _pool_layer(x, mod[3], norm_g[3], pl_w_in_b, pl_w[0], pl_b[0], pl_scale[0],
                         pl_w_out_b, final_g)
    return out
```
